```python
import jax, jax.numpy as jnp
from jax import lax
import numpy as np

D_MODEL = 2048
BATCH = 1
SEQ = 16384
DEPTH = 4
DEC_BATCH = 4
DEC_SEQ = 8192
PAST_LEN = 128

HEAD_DIM = 128
A_Q_HEADS = 4
A_KV_HEADS = 2
A_HALF_WINDOW = 128
B_GROUPS = ((128, 1), (512, 4), (2048, 16))
B_HEADS_PER_GROUP = 2
B_HEADS = B_HEADS_PER_GROUP * len(B_GROUPS)
C_Q_HEADS = 6
C_KV_HEADS = 2
C_BLOCK = 128
GRID_W = 64
ROPE_THETA = 10000.0
D_FF = 5632
N_BRANCH = 3
NORM_EPS = 1e-6
MASK_VALUE = -1e30

A_Q = A_Q_HEADS * HEAD_DIM
A_KV = A_KV_HEADS * HEAD_DIM
B_W = B_HEADS * HEAD_DIM
C_Q = C_Q_HEADS * HEAD_DIM
C_KV = C_KV_HEADS * HEAD_DIM
A_WIDTH = A_Q + 2 * A_KV
B_WIDTH = 3 * B_W
C_WIDTH = C_Q + 2 * C_KV
IN_WIDTH = A_WIDTH + B_WIDTH + C_WIDTH
MIX_WIDTH = A_Q + B_W + C_Q

kernel_name = "hybrid_gated_window_dilated_axial_encoder"


def rms_norm(x, g):
    xf = x.astype(jnp.float32)
    y = xf * lax.rsqrt(jnp.mean(xf * xf, axis=-1, keepdims=True) + NORM_EPS)
    return (y * g.astype(jnp.float32)).astype(x.dtype)


def rope_angles(pos, dim):
    inv = ROPE_THETA ** (-jnp.arange(0, dim, 2, dtype=jnp.float32) / dim)
    return pos.astype(jnp.float32)[:, None] * inv[None, :]


def apply_rope(x, ang):
    cos = jnp.cos(ang)[None, :, None, :]
    sin = jnp.sin(ang)[None, :, None, :]
    xf = x.astype(jnp.float32)
    x1, x2 = jnp.split(xf, 2, axis=-1)
    out = jnp.concatenate([x1 * cos - x2 * sin, x2 * cos + x1 * sin], axis=-1)
    return out.astype(x.dtype)


def apply_axial_rope(x, ang_row, ang_col):
    half = HEAD_DIM // 2
    return jnp.concatenate([apply_rope(x[..., :half], ang_row), apply_rope(x[..., half:], ang_col)], axis=-1)


def swiglu(x, w_in, w_out):
    gate, up = jnp.split(x @ w_in, 2, axis=-1)
    return (jax.nn.silu(gate) * up) @ w_out


def banded_attention(q, k, v, half_window, sink=None):
    n, L, hkv, g, dh = q.shape
    blk = half_window
    nb = -(-L // blk)
    lp = nb * blk
    padq = lp - L
    qp = jnp.pad(q, ((0, 0), (0, padq), (0, 0), (0, 0), (0, 0)))
    kp = jnp.pad(k, ((0, 0), (blk, padq + blk), (0, 0), (0, 0)))
    vp = jnp.pad(v, ((0, 0), (blk, padq + blk), (0, 0), (0, 0)))
    qb = qp.reshape(n, nb, blk, hkv, g, dh)
    kb = kp.reshape(n, nb + 2, blk, hkv, dh)
    vb = vp.reshape(n, nb + 2, blk, hkv, dh)
    kw = jnp.concatenate([kb[:, :-2], kb[:, 1:-1], kb[:, 2:]], axis=2)
    vw = jnp.concatenate([vb[:, :-2], vb[:, 1:-1], vb[:, 2:]], axis=2)
    s = jnp.einsum('nbqhgd,nbkhd->nbhgqk', qb, kw, preferred_element_type=jnp.float32) * (dh ** -0.5)
    qpos = jnp.arange(nb)[:, None, None] * blk + jnp.arange(blk)[None, :, None]
    kpos = jnp.arange(nb)[:, None, None] * blk - blk + jnp.arange(3 * blk)[None, None, :]
    valid = (jnp.abs(qpos - kpos) <= half_window) & (kpos >= 0) & (kpos < L)
    s = jnp.where(valid[None, :, None, None], s, MASK_VALUE)
    lse = jax.nn.logsumexp(s, axis=-1)
    if sink is not None:
        lse = jnp.logaddexp(lse, sink.astype(jnp.float32)[None, None, :, :, None])
    p = jnp.exp(s - lse[..., None])
    out = jnp.einsum('nbhgqk,nbkhd->nbqhgd', p.astype(v.dtype), vw)
    out = out.reshape(n, lp, hkv, g, dh)[:, :L]
    lse = lse.transpose(0, 1, 4, 2, 3).reshape(n, lp, hkv, g)[:, :L]
    return out, lse


def mixer_a(q, k, v, sink, ang):
    b, s = q.shape[:2]
    q = apply_rope(q, ang)
    k = apply_rope(k, ang)
    q5 = q.reshape(b, s, A_KV_HEADS, A_Q_HEADS // A_KV_HEADS, HEAD_DIM)
    out, _ = banded_attention(q5, k, v, A_HALF_WINDOW, sink)
    return out.reshape(b, s, A_Q)


def mixer_b(q, k, v, ang):
    b, s = q.shape[:2]
    q = apply_rope(q, ang)
    k = apply_rope(k, ang)
    hpg = B_HEADS_PER_GROUP
    outs, lses = [], []
    for gi, (window, dil) in enumerate(B_GROUPS):
        sl = slice(gi * hpg, (gi + 1) * hpg)

        def to_residue(t):
            t = t[:, :, sl].reshape(b, s // dil, dil, hpg, HEAD_DIM).transpose(0, 2, 1, 3, 4)
            return t.reshape(b * dil, s // dil, hpg, HEAD_DIM)

        o, l = banded_attention(to_residue(q)[:, :, :, None], to_residue(k), to_residue(v), window // (2 * dil))
        o = o.reshape(b, dil, s // dil, hpg, HEAD_DIM).transpose(0, 2, 1, 3, 4).reshape(b, s, hpg, HEAD_DIM)
        l = l.reshape(b, dil, s // dil, hpg).transpose(0, 2, 1, 3).reshape(b, s, hpg)
        outs.append(o)
        lses.append(l)
    lse = jnp.stack(lses, axis=2)
    wts = jax.nn.softmax(lse, axis=2)
    o = jnp.stack(outs, axis=2) * wts[..., None].astype(q.dtype)
    return o.reshape(b, s, B_W)


def mixer_c(q, k, v, gq, gk, ang_row, ang_col):
    b, s = q.shape[:2]
    q = apply_axial_rope(rms_norm(q, gq), ang_row, ang_col)
    k = apply_axial_rope(rms_norm(k, gk), ang_row, ang_col)
    nb = s // C_BLOCK
    grp = C_Q_HEADS // C_KV_HEADS
    qblocks = q.reshape(b, nb, C_BLOCK, C_KV_HEADS, grp, HEAD_DIM).transpose(1, 0, 2, 3, 4, 5)
    scale = HEAD_DIM ** -0.5

    def attend(qblk):
        sc = jnp.einsum('bqhgd,bkhd->bhgqk', qblk, k, preferred_element_type=jnp.float32) * scale
        p = jax.nn.softmax(sc, axis=-1)
        return jnp.einsum('bhgqk,bkhd->bqhgd', p.astype(v.dtype), v)

    out = lax.map(attend, qblocks)
    return out.transpose(1, 0, 2, 3, 4, 5).reshape(b, s, C_Q)


def encoder_layer(x, g_ffn1, w_ffn1_in, w_ffn1_out, g_mix, w_in, a_sink, c_q_norm, c_k_norm,
                  w_branch, w_gate, b_gate, w_o, g_ffn2, w_ffn2_in, w_ffn2_out, ang, ang_row, ang_col):
    b, s, _ = x.shape
    h = x + 0.5 * swiglu(rms_norm(x, g_ffn1), w_ffn1_in, w_ffn1_out)
    u = rms_norm(h, g_mix)
    proj = u @ w_in
    o = 0
    def take(width, heads):
        return proj[..., o:o + width].reshape(b, s, heads, HEAD_DIM)
    qa = take(A_Q, A_Q_HEADS); o += A_Q
    ka = take(A_KV, A_KV_HEADS); o += A_KV
    va = take(A_KV, A_KV_HEADS); o += A_KV
    qb = take(B_W, B_HEADS); o += B_W
    kb = take(B_W, B_HEADS); o += B_W
    vb = take(B_W, B_HEADS); o += B_W
    qc = take(C_Q, C_Q_HEADS); o += C_Q
    kc = take(C_KV, C_KV_HEADS); o += C_KV
    vc = take(C_KV, C_KV_HEADS)
    ya = mixer_a(qa, ka, va, a_sink, ang)
    yb = mixer_b(qb, kb, vb, ang)
    yc = mixer_c(qc, kc, vc, c_q_norm, c_k_norm, ang_row, ang_col)
    br_a = ya @ w_branch[:A_Q]
    br_b = yb @ w_branch[A_Q:A_Q + B_W]
    br_c = yc @ w_branch[A_Q + B_W:]
    gates = jax.nn.sigmoid(u @ w_gate + b_gate).reshape(b, s, N_BRANCH, D_MODEL)
    merged = gates[:, :, 0] * br_a + gates[:, :, 1] * br_b + gates[:, :, 2] * br_c
    h = h + merged @ w_o
    return h + 0.5 * swiglu(rms_norm(h, g_ffn2), w_ffn2_in, w_ffn2_out)


def run_trunk(x, g_ffn1, w_ffn1_in, w_ffn1_out, g_mix, w_in, a_sink, c_q_norm, c_k_norm,
              w_branch, w_gate, b_gate, w_o, g_ffn2, w_ffn2_in, w_ffn2_out, g_final):
    s = x.shape[1]
    pos = jnp.arange(s)
    ang = rope_angles(pos, HEAD_DIM)
    rows = s // GRID_W
    row_id = jnp.repeat(jnp.arange(rows), GRID_W)
    col_id = jnp.tile(jnp.arange(GRID_W), rows)
    ang_row = rope_angles(row_id, HEAD_DIM // 2)
    ang_col = rope_angles(col_id, HEAD_DIM // 2)
    for l in range(DEPTH):
        x = encoder_layer(x, g_ffn1[l], w_ffn1_in[l], w_ffn1_out[l], g_mix[l], w_in[l], a_sink[l],
                          c_q_norm[l], c_k_norm[l], w_branch[l], w_gate[l], b_gate[l], w_o[l],
                          g_ffn2[l], w_ffn2_in[l], w_ffn2_out[l], ang, ang_row, ang_col)
    return rms_norm(x, g_final)


def setup_inputs(seed: int = 0) -> dict:
    key = jax.random.key(seed)
    ks = jax.random.split(key, 24)
    f32 = jnp.float32

    def nrm(k, shape, fan):
        return jax.random.normal(k, shape, f32) * (fan ** -0.5)

    def gain(k, shape):
        return 1.0 + 0.02 * jax.random.normal(k, shape, f32)

    w_branch = jnp.concatenate([
        nrm(ks[10], (DEPTH, A_Q, D_MODEL), A_Q),
        nrm(ks[11], (DEPTH, B_W, D_MODEL), B_W),
        nrm(ks[12], (DEPTH, C_Q, D_MODEL), C_Q)], axis=1)
    return {
        "x_prompt": jax.random.normal(ks[0], (BATCH, SEQ, D_MODEL), f32),
        "x_sample": jax.random.normal(ks[1], (DEC_BATCH, DEC_SEQ, D_MODEL), f32),
        "g_ffn1": gain(ks[2], (DEPTH, D_MODEL)),
        "w_ffn1_in": nrm(ks[3], (DEPTH, D_MODEL, 2 * D_FF), D_MODEL),
        "w_ffn1_out": nrm(ks[4], (DEPTH, D_FF, D_MODEL), D_FF),
        "g_mix": gain(ks[5], (DEPTH, D_MODEL)),
        "w_in": nrm(ks[6], (DEPTH, D_MODEL, IN_WIDTH), D_MODEL),
        "a_sink": 0.1 * jax.random.normal(ks[7], (DEPTH, A_KV_HEADS, A_Q_HEADS // A_KV_HEADS), f32),
        "c_q_norm": gain(ks[8], (DEPTH, HEAD_DIM)),
        "c_k_norm": gain(ks[9], (DEPTH, HEAD_DIM)),
        "w_branch": w_branch,
        "w_gate": nrm(ks[13], (DEPTH, D_MODEL, N_BRANCH * D_MODEL), D_MODEL),
        "b_gate": 0.1 * jax.random.normal(ks[14], (DEPTH, N_BRANCH * D_MODEL), f32),
        "w_o": nrm(ks[15], (DEPTH, D_MODEL, D_MODEL), D_MODEL),
        "g_ffn2": gain(ks[16], (DEPTH, D_MODEL)),
        "w_ffn2_in": nrm(ks[17], (DEPTH, D_MODEL, 2 * D_FF), D_MODEL),
        "w_ffn2_out": nrm(ks[18], (DEPTH, D_FF, D_MODEL), D_FF),
        "g_final": gain(ks[19], (D_MODEL,)),
    }


def reference(x_prompt, x_sample, g_ffn1, w_ffn1_in, w_ffn1_out, g_mix, w_in, a_sink, c_q_norm, c_k_norm,
              w_branch, w_gate, b_gate, w_o, g_ffn2, w_ffn2_in, w_ffn2_out, g_final):
    y_prompt = run_trunk(x_prompt, g_ffn1, w_ffn1_in, w_ffn1_out, g_mix, w_in, a_sink, c_q_norm, c_k_norm,
                         w_branch, w_gate, b_gate, w_o, g_ffn2, w_ffn2_in, w_ffn2_out, g_final)
    y_sample = run_trunk(x_sample, g_ffn1, w_ffn1_in, w_ffn1_out, g_mix, w_in, a_sink, c_q_norm, c_k_norm,
                         w_branch, w_gate, b_gate, w_o, g_ffn2, w_ffn2_in, w_ffn2_out, g_final)
    return (y_prompt, y_sample)
```

```python
import functools

import jax
import jax.numpy as jnp
from jax import lax
from jax.experimental import pallas as pl
from jax.experimental.pallas import tpu as pltpu

F32 = jnp.float32
BF16 = jnp.bfloat16

D_MODEL = 2048
D_FF = 5632
HEAD_DIM = 128
A_Q_HEADS, A_KV_HEADS, A_HALF_WINDOW = 4, 2, 128
B_GROUPS = ((128, 1), (512, 4), (2048, 16))
B_HEADS_PER_GROUP = 2
C_Q_HEADS, C_KV_HEADS = 6, 2
GRID_W = 64
ROPE_THETA = 10000.0
N_BRANCH = 3
NORM_EPS = 1e-6
MASK_VALUE = -1e30
Q_SCALE = HEAD_DIM ** -0.5

A_Q = A_Q_HEADS * HEAD_DIM
A_KV = A_KV_HEADS * HEAD_DIM
B_W = B_HEADS_PER_GROUP * len(B_GROUPS) * HEAD_DIM
C_Q = C_Q_HEADS * HEAD_DIM
C_KV = C_KV_HEADS * HEAD_DIM
A_WIDTH = A_Q + 2 * A_KV
B_GROUP_WIDTH = 3 * B_HEADS_PER_GROUP * HEAD_DIM
IN_WIDTH = A_WIDTH + 3 * B_W + C_Q + 2 * C_KV

VMEM_LIMIT_BYTES = 58 * 1024 * 1024

TOKEN_BLOCK = 512
FF_CHUNK = 512
PROJ_CHUNK = 512
MIX_CHUNK = 256
BAND_ROWS = 512
FLASH_Q_ROWS = 256
FLASH_K_ROWS = 512


def _params(*sem):
    return pltpu.CompilerParams(dimension_semantics=sem, vmem_limit_bytes=VMEM_LIMIT_BYTES)


def _rms(x, g):
    return x * lax.rsqrt(jnp.mean(x * x, axis=-1, keepdims=True) + NORM_EPS) * g


def _ffn_kernel(x_ref, g_ref, wg_ref, wu_ref, wo_ref, gf_ref, o_ref, n_ref, *, final_norm):
    j = pl.program_id(1)

    @pl.when(j == 0)
    def _():
        x = x_ref[...]
        n_ref[...] = _rms(x, g_ref[...]).astype(BF16)
        o_ref[...] = x

    n = n_ref[...]
    gate = jnp.dot(n, wg_ref[...], preferred_element_type=F32)
    up = jnp.dot(n, wu_ref[...], preferred_element_type=F32)
    a = (gate * jax.nn.sigmoid(gate) * (0.5 * up)).astype(BF16)
    o_ref[...] += jnp.dot(a, wo_ref[...], preferred_element_type=F32)

    if final_norm:
        @pl.when(j == pl.num_programs(1) - 1)
        def _():
            o_ref[...] = _rms(o_ref[...], gf_ref[...])


def _ffn(x, g, w_in, w_out, g_final, *, final_norm):
    t, d = x.shape
    tm = min(TOKEN_BLOCK, t)
    nf = D_FF // FF_CHUNK
    return pl.pallas_call(
        functools.partial(_ffn_kernel, final_norm=final_norm),
        grid=(t // tm, nf),
        in_specs=[
            pl.BlockSpec((tm, d), lambda i, j: (i, 0)),
            pl.BlockSpec((1, d), lambda i, j: (0, 0)),
            pl.BlockSpec((d, FF_CHUNK), lambda i, j: (0, j)),
            pl.BlockSpec((d, FF_CHUNK), lambda i, j: (0, j + nf)),
            pl.BlockSpec((FF_CHUNK, d), lambda i, j: (j, 0)),
            pl.BlockSpec((1, d), lambda i, j: (0, 0)),
        ],
        out_specs=pl.BlockSpec((tm, d), lambda i, j: (i, 0)),
        out_shape=jax.ShapeDtypeStruct((t, d), F32),
        scratch_shapes=[pltpu.VMEM((tm, d), BF16)],
        compiler_params=_params("parallel", "arbitrary"),
        name="ffn_final" if final_norm else "ffn",
    )(x, g, w_in, w_in, w_out, g_final)


def _rope(y, cos, sin):
    return y * cos + pltpu.roll(y, HEAD_DIM // 2, 1) * sin


def _axial_rope(y, cos, sin_lo, sin_hi):
    q = HEAD_DIM // 4
    return y * cos + pltpu.roll(y, HEAD_DIM - q, 1) * sin_lo + pltpu.roll(y, q, 1) * sin_hi


def _proj_layout():
    heads = []
    hpg = B_HEADS_PER_GROUP
    for h in range(A_Q_HEADS):
        heads.append(("rope_q", 0, h * HEAD_DIM))
    for h in range(A_KV_HEADS):
        heads.append(("rope", 0, A_Q + h * HEAD_DIM))
    for h in range(A_KV_HEADS):
        heads.append(("plain", 0, A_Q + A_KV + h * HEAD_DIM))
    for part, kind in enumerate(("rope_q", "rope", "plain")):
        for h in range(hpg * len(B_GROUPS)):
            heads.append((kind, 1 + h // hpg, (part * hpg + h % hpg) * HEAD_DIM))
    for h in range(C_Q_HEADS):
        heads.append(("axial_q", 4, h * HEAD_DIM))
    for h in range(C_KV_HEADS):
        heads.append(("axial_k", 5, h * HEAD_DIM))
    for h in range(C_KV_HEADS):
        heads.append(("plain", 5, C_KV + h * HEAD_DIM))
    assert len(heads) * HEAD_DIM == IN_WIDTH
    return heads


def _proj_kernel(h_ref, g_ref, w_ref, gq_ref, gk_ref, cos_ref, sin_ref, cosc_ref, sinlo_ref, sinhi_ref,
                 a_ref, b0_ref, b1_ref, b2_ref, cq_ref, ckv_ref):
    outs = (a_ref, b0_ref, b1_ref, b2_ref, cq_ref, ckv_ref)
    u = _rms(h_ref[...], g_ref[...]).astype(BF16)
    layout = _proj_layout()
    heads_per_chunk = PROJ_CHUNK // HEAD_DIM
    for c in range(IN_WIDTH // PROJ_CHUNK):
        p = jnp.dot(u, w_ref[:, c * PROJ_CHUNK:(c + 1) * PROJ_CHUNK], preferred_element_type=F32)
        for hh in range(heads_per_chunk):
            kind, oi, col = layout[c * heads_per_chunk + hh]
            y = p[:, hh * HEAD_DIM:(hh + 1) * HEAD_DIM]
            if kind in ("rope", "rope_q"):
                y = _rope(y, cos_ref[...], sin_ref[...])
            elif kind in ("axial_q", "axial_k"):
                y = _rms(y, gq_ref[...] if kind == "axial_q" else gk_ref[...])
                y = _axial_rope(y, cosc_ref[...], sinlo_ref[...], sinhi_ref[...])
            if kind.endswith("_q"):
                y = y * Q_SCALE
            outs[oi][:, col:col + HEAD_DIM] = y.astype(BF16)


def _proj(h, g, w, gq, gk, tables, seq):
    t, d = h.shape
    tm = min(TOKEN_BLOCK, seq)
    nseq = seq // tm
    row = lambda i: (i, 0)
    const = lambda i: (0, 0)
    pos = lambda i: (i % nseq, 0)
    widths = (A_WIDTH, B_GROUP_WIDTH, B_GROUP_WIDTH, B_GROUP_WIDTH, C_Q, 2 * C_KV)
    return pl.pallas_call(
        _proj_kernel,
        grid=(t // tm,),
        in_specs=[
            pl.BlockSpec((tm, d), row),
            pl.BlockSpec((1, d), const),
            pl.BlockSpec((d, IN_WIDTH), const, pipeline_mode=pl.Buffered(1)),
            pl.BlockSpec((1, HEAD_DIM), const),
            pl.BlockSpec((1, HEAD_DIM), const),
        ] + [pl.BlockSpec((tm, HEAD_DIM), pos)] * 5,
        out_specs=[pl.BlockSpec((tm, w_), row) for w_ in widths],
        out_shape=[jax.ShapeDtypeStruct((t, w_), BF16) for w_ in widths],
        compiler_params=_params("parallel"),
        name="proj",
    )(h, g, w, gq, gk, *tables)


def _band_kernel(*refs, W, G, HKV, TQ, L, has_sink, want_lse):
    refs = list(refs)
    main_ref, prev_ref, next_ref = refs[:3]
    refs = refs[3:]
    sink_ref = refs.pop(0) if has_sink else None
    o_ref = refs.pop(0)
    lse_ref = refs.pop(0) if want_lse else None

    i = pl.program_id(2)
    nsub = TQ // W
    qi = lax.broadcasted_iota(jnp.int32, (W, 3 * W), 0)
    ci = lax.broadcasted_iota(jnp.int32, (W, 3 * W), 1)
    band = jnp.abs(ci - W - qi) <= W

    def head(ref, idx):
        return ref[0, :, idx * HEAD_DIM:(idx + 1) * HEAD_DIM]

    for h in range(HKV):
        kh, vh = HKV * G + h, HKV * G + HKV + h
        kcat = jnp.concatenate([head(prev_ref, kh), head(main_ref, kh), head(next_ref, kh)], axis=0)
        vcat = jnp.concatenate([head(prev_ref, vh), head(main_ref, vh), head(next_ref, vh)], axis=0)
        for j in range(nsub):
            kw = kcat[j * W:(j + 3) * W]
            vw = vcat[j * W:(j + 3) * W]
            valid = band
            if j == 0 or j == nsub - 1:
                kpos = ci + (i * TQ + (j - 1) * W)
                valid = jnp.logical_and(valid, jnp.logical_and(kpos >= 0, kpos < L))
            qs, ms = [], []
            for g in range(G):
                col = (h * G + g) * HEAD_DIM
                qs.append(main_ref[0, j * W:(j + 1) * W, col:col + HEAD_DIM])
                ms.append(valid)
            q = jnp.concatenate(qs, axis=0) if G > 1 else qs[0]
            valid_g = jnp.concatenate(ms, axis=0) if G > 1 else ms[0]
            s = lax.dot_general(q, kw, (((1,), (1,)), ((), ())), preferred_element_type=F32)
            s = jnp.where(valid_g, s, MASK_VALUE)
            m = jnp.max(s, axis=-1, keepdims=True)
            p = jnp.exp(s - m)
            l = jnp.sum(p, axis=-1, keepdims=True)
            if has_sink:
                sink = jnp.concatenate(
                    [jnp.full((W, 1), sink_ref[h * G + g], F32) for g in range(G)], axis=0)
                l = l + jnp.exp(sink - m)
            o = jnp.dot(p.astype(BF16), vw, preferred_element_type=F32) / l
            for g in range(G):
                col = (h * G + g) * HEAD_DIM
                o_ref[0, j * W:(j + 1) * W, col:col + HEAD_DIM] = o[g * W:(g + 1) * W].astype(o_ref.dtype)
                if want_lse:
                    lse = (m + jnp.log(l))[g * W:(g + 1) * W]
                    lse_ref[0, j * W:(j + 1) * W, col:col + HEAD_DIM] = jnp.broadcast_to(lse, (W, HEAD_DIM))


def _band(x, sink, *, W, G, HKV, R, want_lse):
    bsz, L, _ = x.shape
    cw = (HKV * G + 2 * HKV) * HEAD_DIM
    ow = HKV * G * HEAD_DIM
    tq = min(BAND_ROWS, L)
    nblk, per = L // tq, tq // W
    has_sink = sink is not None
    in_specs = [
        pl.BlockSpec((1, tq, cw), lambda b, r, i: (b, i, r)),
        pl.BlockSpec((1, W, cw), lambda b, r, i: (b, jnp.maximum(i * per - 1, 0), r)),
        pl.BlockSpec((1, W, cw), lambda b, r, i: (b, jnp.minimum((i + 1) * per, L // W - 1), r)),
    ]
    args = [x, x, x]
    if has_sink:
        in_specs.append(pl.BlockSpec(memory_space=pltpu.SMEM))
        args.append(sink)
    out_spec = pl.BlockSpec((1, tq, ow), lambda b, r, i: (b, i, r))
    out_specs, out_shape = [out_spec], [jax.ShapeDtypeStruct((bsz, L, R * ow), BF16)]
    if want_lse:
        out_specs.append(out_spec)
        out_shape.append(jax.ShapeDtypeStruct((bsz, L, R * ow), F32))
    return pl.pallas_call(
        functools.partial(_band_kernel, W=W, G=G, HKV=HKV, TQ=tq, L=L, has_sink=has_sink, want_lse=want_lse),
        grid=(bsz, R, nblk),
        in_specs=in_specs,
        out_specs=out_specs,
        out_shape=out_shape,
        compiler_params=_params("parallel", "parallel", "parallel"),
        name="band_sink" if has_sink else f"band_dil{R}",
    )(*args)


def _flash_kernel(q_ref, k_ref, v_ref, o_ref, qs_ref, m_ref, l_ref, acc_ref, *, TQ, TK, G, S):
    for g in range(G):
        qs_ref[g * TQ:(g + 1) * TQ, :] = q_ref[0, :, g * HEAD_DIM:(g + 1) * HEAD_DIM]
    m_ref[...] = jnp.full(m_ref.shape, MASK_VALUE, F32)
    l_ref[...] = jnp.zeros(l_ref.shape, F32)
    acc_ref[...] = jnp.zeros(acc_ref.shape, F32)

    def step(ki, carry):
        start = pl.multiple_of(ki * TK, TK)
        k = k_ref[0, pl.ds(start, TK), :]
        v = v_ref[0, pl.ds(start, TK), :]
        s = lax.dot_general(qs_ref[...], k, (((1,), (1,)), ((), ())), preferred_element_type=F32)
        m_prev = m_ref[...]
        m_next = jnp.maximum(m_prev, jnp.max(s, axis=-1, keepdims=True))
        alpha = jnp.exp(m_prev - m_next)
        p = jnp.exp(s - pltpu.repeat(m_next, TK // HEAD_DIM, 1))
        l_ref[...] = alpha * l_ref[...] + jnp.sum(p, axis=-1, keepdims=True)
        acc_ref[...] = alpha * acc_ref[...] + jnp.dot(p.astype(BF16), v, preferred_element_type=F32)
        m_ref[...] = m_next
        return carry

    lax.fori_loop(0, S // TK, step, 0)
    out = acc_ref[...] / l_ref[...]
    for g in range(G):
        o_ref[0, :, g * HEAD_DIM:(g + 1) * HEAD_DIM] = out[g * TQ:(g + 1) * TQ].astype(BF16)


def _flash(q, kv):
    bsz, S, _ = q.shape
    G = C_Q_HEADS // C_KV_HEADS
    tq = min(FLASH_Q_ROWS, S)
    tk = min(FLASH_K_ROWS, S)
    gw = G * HEAD_DIM
    return pl.pallas_call(
        functools.partial(_flash_kernel, TQ=tq, TK=tk, G=G, S=S),
        grid=(bsz, C_KV_HEADS, S // tq),
        in_specs=[
            pl.BlockSpec((1, tq, gw), lambda b, h, i: (b, i, h)),
            pl.BlockSpec((1, S, HEAD_DIM), lambda b, h, i: (b, 0, h)),
            pl.BlockSpec((1, S, HEAD_DIM), lambda b, h, i: (b, 0, C_KV_HEADS + h)),
        ],
        out_specs=pl.BlockSpec((1, tq, gw), lambda b, h, i: (b, i, h)),
        out_shape=jax.ShapeDtypeStruct((bsz, S, C_Q), BF16),
        scratch_shapes=[
            pltpu.VMEM((G * tq, HEAD_DIM), BF16),
            pltpu.VMEM((G * tq, HEAD_DIM), F32),
            pltpu.VMEM((G * tq, HEAD_DIM), F32),
            pltpu.VMEM((G * tq, HEAD_DIM), F32),
        ],
        compiler_params=_params("parallel", "parallel", "arbitrary"),
        name="flash",
    )(q, kv, kv)


def _mix_kernel(h_ref, g_ref, ya_ref, ob0_ref, ob1_ref, ob2_ref, ls0_ref, ls1_ref, ls2_ref, yc_ref,
                wg0_ref, wg1_ref, wg2_ref, bg0_ref, bg1_ref, bg2_ref, wba_ref, wbb_ref, wbc_ref, wo_ref,
                o_ref, u_ref, yb_ref):
    c = pl.program_id(1)

    @pl.when(c == 0)
    def _():
        x = h_ref[...]
        u_ref[...] = _rms(x, g_ref[...]).astype(BF16)
        o_ref[...] = x
        obs = (ob0_ref, ob1_ref, ob2_ref)
        lss = (ls0_ref, ls1_ref, ls2_ref)
        gw = B_HEADS_PER_GROUP * HEAD_DIM
        for j in range(B_HEADS_PER_GROUP):
            sl = slice(j * HEAD_DIM, (j + 1) * HEAD_DIM)
            ls = [r[:, sl] for r in lss]
            m = jnp.maximum(jnp.maximum(ls[0], ls[1]), ls[2])
            es = [jnp.exp(v - m) for v in ls]
            den = es[0] + es[1] + es[2]
            for gi in range(len(B_GROUPS)):
                wt = es[gi] / den
                yb_ref[:, gi * gw + j * HEAD_DIM:gi * gw + (j + 1) * HEAD_DIM] = (
                    obs[gi][:, sl].astype(F32) * wt).astype(BF16)

    u = u_ref[...]

    def gate(w_ref, b_ref):
        return jax.nn.sigmoid(jnp.dot(u, w_ref[...], preferred_element_type=F32) + b_ref[...])

    merged = gate(wg0_ref, bg0_ref) * jnp.dot(ya_ref[...], wba_ref[...], preferred_element_type=F32)
    merged += gate(wg1_ref, bg1_ref) * jnp.dot(yb_ref[...], wbb_ref[...], preferred_element_type=F32)
    merged += gate(wg2_ref, bg2_ref) * jnp.dot(yc_ref[...], wbc_ref[...], preferred_element_type=F32)
    o_ref[...] += jnp.dot(merged.astype(BF16), wo_ref[...], preferred_element_type=F32)


def _mix(h, g, ya, obs, lss, yc, w_gate, b_gate, wb_a, wb_b, wb_c, w_o):
    t, d = h.shape
    tm = min(TOKEN_BLOCK, t)
    nc = d // MIX_CHUNK
    row = lambda i, c: (i, 0)
    gw = B_HEADS_PER_GROUP * HEAD_DIM
    gate_spec = lambda k: pl.BlockSpec((d, MIX_CHUNK), lambda i, c: (0, k * nc + c))
    bias_spec = lambda k: pl.BlockSpec((1, MIX_CHUNK), lambda i, c: (0, k * nc + c))
    col_spec = lambda rows: pl.BlockSpec((rows, MIX_CHUNK), lambda i, c: (0, c))
    return pl.pallas_call(
        _mix_kernel,
        grid=(t // tm, nc),
        in_specs=[
            pl.BlockSpec((tm, d), row),
            pl.BlockSpec((1, d), lambda i, c: (0, 0)),
            pl.BlockSpec((tm, A_Q), row),
            pl.BlockSpec((tm, gw), row), pl.BlockSpec((tm, gw), row), pl.BlockSpec((tm, gw), row),
            pl.BlockSpec((tm, gw), row), pl.BlockSpec((tm, gw), row), pl.BlockSpec((tm, gw), row),
            pl.BlockSpec((tm, C_Q), row),
            gate_spec(0), gate_spec(1), gate_spec(2),
            bias_spec(0), bias_spec(1), bias_spec(2),
            col_spec(A_Q), col_spec(B_W), col_spec(C_Q),
            pl.BlockSpec((MIX_CHUNK, d), lambda i, c: (c, 0)),
        ],
        out_specs=pl.BlockSpec((tm, d), row),
        out_shape=jax.ShapeDtypeStruct((t, d), F32),
        scratch_shapes=[pltpu.VMEM((tm, d), BF16), pltpu.VMEM((tm, B_W), BF16)],
        compiler_params=_params("parallel", "arbitrary"),
        name="mix",
    )(h, g, ya, *obs, *lss, yc, w_gate, w_gate, w_gate, b_gate, b_gate, b_gate, wb_a, wb_b, wb_c, w_o)


def _rope_tables(seq):
    def angles(pos, dim):
        inv = ROPE_THETA ** (-jnp.arange(0, dim, 2, dtype=F32) / dim)
        return pos.astype(F32)[:, None] * inv[None, :]

    pos = jnp.arange(seq)
    ang = angles(pos, HEAD_DIM)
    cos, sin = jnp.cos(ang), jnp.sin(ang)
    ar = angles(pos // GRID_W, HEAD_DIM // 2)
    ac = angles(pos % GRID_W, HEAD_DIM // 2)
    cr, sr, cc, sc = jnp.cos(ar), jnp.sin(ar), jnp.cos(ac), jnp.sin(ac)
    z = jnp.zeros_like(sr)
    return (
        jnp.concatenate([cos, cos], -1),
        jnp.concatenate([-sin, sin], -1),
        jnp.concatenate([cr, cr, cc, cc], -1),
        jnp.concatenate([-sr, z, -sc, z], -1),
        jnp.concatenate([z, sr, z, sc], -1),
    )


def _layer_weights(l, g_ffn1, w_ffn1_in, w_ffn1_out, g_mix, w_in, a_sink, c_q_norm, c_k_norm,
                   w_branch, w_gate, b_gate, w_o, g_ffn2, w_ffn2_in, w_ffn2_out):
    bf = lambda w: w.astype(BF16)
    vec = lambda v: v.reshape(1, -1)
    return dict(
        g_ffn1=vec(g_ffn1[l]), w_ffn1_in=bf(w_ffn1_in[l]), w_ffn1_out=bf(w_ffn1_out[l]),
        g_mix=vec(g_mix[l]), w_in=bf(w_in[l]), a_sink=a_sink[l].reshape(-1),
        c_q_norm=vec(c_q_norm[l]), c_k_norm=vec(c_k_norm[l]),
        wb_a=bf(w_branch[l, :A_Q]), wb_b=bf(w_branch[l, A_Q:A_Q + B_W]), wb_c=bf(w_branch[l, A_Q + B_W:]),
        w_gate=bf(w_gate[l]), b_gate=vec(b_gate[l]), w_o=bf(w_o[l]),
        g_ffn2=vec(g_ffn2[l]), w_ffn2_in=bf(w_ffn2_in[l]), w_ffn2_out=bf(w_ffn2_out[l]),
    )


def _trunk(x, layers, g_final):
    bsz, seq, d = x.shape
    t = bsz * seq
    tables = _rope_tables(seq)
    gw = B_HEADS_PER_GROUP * HEAD_DIM
    x = x.reshape(t, d)
    for li, w in enumerate(layers):
        h = _ffn(x, w["g_ffn1"], w["w_ffn1_in"], w["w_ffn1_out"], g_final, final_norm=False)
        a, b0, b1, b2, cq, ckv = _proj(h, w["g_mix"], w["w_in"], w["c_q_norm"], w["c_k_norm"], tables, seq)
        (ya,) = _band(a.reshape(bsz, seq, A_WIDTH), w["a_sink"], W=A_HALF_WINDOW,
                      G=A_Q_HEADS // A_KV_HEADS, HKV=A_KV_HEADS, R=1, want_lse=False)
        obs, lss = [], []
        for bg, (window, dil) in zip((b0, b1, b2), B_GROUPS):
            o, ls = _band(bg.reshape(bsz, seq // dil, dil * B_GROUP_WIDTH), None, W=window // (2 * dil),
                          G=1, HKV=B_HEADS_PER_GROUP, R=dil, want_lse=True)
            obs.append(o.reshape(t, gw))
            lss.append(ls.reshape(t, gw))
        yc = _flash(cq.reshape(bsz, seq, C_Q), ckv.reshape(bsz, seq, 2 * C_KV))
        h = _mix(h, w["g_mix"], ya.reshape(t, A_Q), obs, lss, yc.reshape(t, C_Q),
                 w["w_gate"], w["b_gate"], w["wb_a"], w["wb_b"], w["wb_c"], w["w_o"])
        x = _ffn(h, w["g_ffn2"], w["w_ffn2_in"], w["w_ffn2_out"], g_final,
                 final_norm=(li == len(layers) - 1))
    return x.reshape(bsz, seq, d)


def kernel(x_prompt, x_sample, g_ffn1, w_ffn1_in, w_ffn1_out, g_mix, w_in, a_sink, c_q_norm, c_k_norm,
           w_branch, w_gate, b_gate, w_o, g_ffn2, w_ffn2_in, w_ffn2_out, g_final):
    depth = g_ffn1.shape[0]
    layers = [
        _layer_weights(l, g_ffn1, w_ffn1_in, w_ffn1_out, g_mix, w_in, a_sink, c_q_norm, c_k_norm,
                       w_branch, w_gate, b_gate, w_o, g_ffn2, w_ffn2_in, w_ffn2_out)
        for l in range(depth)
    ]
    gf = g_final.reshape(1, -1)
    return _trunk(x_prompt, layers, gf), _trunk(x_sample, layers, gf)
```

```python
import functools

import jax
import jax.numpy as jnp
from jax import lax
from jax.experimental import pallas as pl
from jax.experimental.pallas import tpu as pltpu

F32 = jnp.float32
BF16 = jnp.bfloat16

D_MODEL = 2048
D_FF = 5632
HEAD_DIM = 128
A_Q_HEADS, A_KV_HEADS, A_HALF_WINDOW = 4, 2, 128
B_GROUPS = ((128, 1), (512, 4), (2048, 16))
B_HEADS_PER_GROUP = 2
C_Q_HEADS, C_KV_HEADS = 6, 2
GRID_W = 64
ROPE_THETA = 10000.0
N_BRANCH = 3
NORM_EPS = 1e-6
MASK_VALUE = -1e30
Q_SCALE = HEAD_DIM ** -0.5
LOG2_E = 1.4426950408889634

A_Q = A_Q_HEADS * HEAD_DIM
A_KV = A_KV_HEADS * HEAD_DIM
B_W = B_HEADS_PER_GROUP * len(B_GROUPS) * HEAD_DIM
C_Q = C_Q_HEADS * HEAD_DIM
C_KV = C_KV_HEADS * HEAD_DIM
A_WIDTH = A_Q + 2 * A_KV
B_GROUP_WIDTH = 3 * B_HEADS_PER_GROUP * HEAD_DIM
IN_WIDTH = A_WIDTH + 3 * B_W + C_Q + 2 * C_KV

VMEM_LIMIT_BYTES = 58 * 1024 * 1024

TOKEN_BLOCK = 512
FF_CHUNK = 512
PROJ_CHUNK = 512
MIX_CHUNK = 512
BAND_ROWS = 512
FLASH_Q_ROWS = 256
FLASH_K_ROWS = 512
FLASH_UNROLL = 8


def _params(*sem):
    return pltpu.CompilerParams(dimension_semantics=sem, vmem_limit_bytes=VMEM_LIMIT_BYTES)


def _rms(x, g):
    return x * lax.rsqrt(jnp.mean(x * x, axis=-1, keepdims=True) + NORM_EPS) * g


def _ffn_kernel(x_ref, g_ref, wg_ref, wu_ref, wo_ref, gf_ref, o_ref, n_ref, *, final_norm):
    j = pl.program_id(1)

    @pl.when(j == 0)
    def _():
        x = x_ref[...]
        n_ref[...] = _rms(x, g_ref[...]).astype(BF16)
        o_ref[...] = x

    n = n_ref[...]
    gate = jnp.dot(n, wg_ref[...], preferred_element_type=F32)
    up = jnp.dot(n, wu_ref[...], preferred_element_type=F32)
    a = (gate * jax.nn.sigmoid(gate) * (0.5 * up)).astype(BF16)
    o_ref[...] += jnp.dot(a, wo_ref[...], preferred_element_type=F32)

    if final_norm:
        @pl.when(j == pl.num_programs(1) - 1)
        def _():
            o_ref[...] = _rms(o_ref[...], gf_ref[...])


def _ffn(x, g, w_in, w_out, g_final, *, final_norm):
    t, d = x.shape
    tm = min(TOKEN_BLOCK, t)
    nf = D_FF // FF_CHUNK
    return pl.pallas_call(
        functools.partial(_ffn_kernel, final_norm=final_norm),
        grid=(t // tm, nf),
        in_specs=[
            pl.BlockSpec((tm, d), lambda i, j: (i, 0)),
            pl.BlockSpec((1, d), lambda i, j: (0, 0)),
            pl.BlockSpec((d, FF_CHUNK), lambda i, j: (0, j)),
            pl.BlockSpec((d, FF_CHUNK), lambda i, j: (0, j + nf)),
            pl.BlockSpec((FF_CHUNK, d), lambda i, j: (j, 0)),
            pl.BlockSpec((1, d), lambda i, j: (0, 0)),
        ],
        out_specs=pl.BlockSpec((tm, d), lambda i, j: (i, 0)),
        out_shape=jax.ShapeDtypeStruct((t, d), F32),
        scratch_shapes=[pltpu.VMEM((tm, d), BF16)],
        compiler_params=_params("parallel", "arbitrary"),
        name="ffn_final" if final_norm else "ffn",
    )(x, g, w_in, w_in, w_out, g_final)


def _rope(y, cos, sin):
    return y * cos + pltpu.roll(y, HEAD_DIM // 2, 1) * sin


def _axial_rope(y, cos, sin_lo, sin_hi):
    q = HEAD_DIM // 4
    return y * cos + pltpu.roll(y, HEAD_DIM - q, 1) * sin_lo + pltpu.roll(y, q, 1) * sin_hi


def _proj_layout():
    heads = []
    hpg = B_HEADS_PER_GROUP
    for h in range(A_Q_HEADS):
        heads.append(("rope_q", 0, h * HEAD_DIM))
    for h in range(A_KV_HEADS):
        heads.append(("rope", 0, A_Q + h * HEAD_DIM))
    for h in range(A_KV_HEADS):
        heads.append(("plain", 0, A_Q + A_KV + h * HEAD_DIM))
    for part, kind in enumerate(("rope_q", "rope", "plain")):
        for h in range(hpg * len(B_GROUPS)):
            heads.append((kind, 1 + h // hpg, (part * hpg + h % hpg) * HEAD_DIM))
    for h in range(C_Q_HEADS):
        heads.append(("axial_q", 4, h * HEAD_DIM))
    for h in range(C_KV_HEADS):
        heads.append(("axial_k", 5, h * HEAD_DIM))
    for h in range(C_KV_HEADS):
        heads.append(("plain", 5, C_KV + h * HEAD_DIM))
    assert len(heads) * HEAD_DIM == IN_WIDTH
    return heads


def _proj_kernel(h_ref, g_ref, w_ref, gq_ref, gk_ref, cos_ref, sin_ref, cosc_ref, sinlo_ref, sinhi_ref,
                 a_ref, b0_ref, b1_ref, b2_ref, cq_ref, ckv_ref):
    outs = (a_ref, b0_ref, b1_ref, b2_ref, cq_ref, ckv_ref)
    u = _rms(h_ref[...], g_ref[...]).astype(BF16)
    layout = _proj_layout()
    heads_per_chunk = PROJ_CHUNK // HEAD_DIM
    for c in range(IN_WIDTH // PROJ_CHUNK):
        p = jnp.dot(u, w_ref[:, c * PROJ_CHUNK:(c + 1) * PROJ_CHUNK], preferred_element_type=F32)
        for hh in range(heads_per_chunk):
            kind, oi, col = layout[c * heads_per_chunk + hh]
            y = p[:, hh * HEAD_DIM:(hh + 1) * HEAD_DIM]
            if kind in ("rope", "rope_q"):
                y = _rope(y, cos_ref[...], sin_ref[...])
            elif kind in ("axial_q", "axial_k"):
                y = _rms(y, gq_ref[...] if kind == "axial_q" else gk_ref[...])
                y = _axial_rope(y, cosc_ref[...], sinlo_ref[...], sinhi_ref[...])
            if kind == "rope_q":
                y = y * Q_SCALE
            elif kind == "axial_q":
                y = y * (Q_SCALE * LOG2_E)
            outs[oi][:, col:col + HEAD_DIM] = y.astype(BF16)


def _proj(h, g, w, gq, gk, tables, seq):
    t, d = h.shape
    tm = min(TOKEN_BLOCK, seq)
    nseq = seq // tm
    row = lambda i: (i, 0)
    const = lambda i: (0, 0)
    pos = lambda i: (i % nseq, 0)
    widths = (A_WIDTH, B_GROUP_WIDTH, B_GROUP_WIDTH, B_GROUP_WIDTH, C_Q, 2 * C_KV)
    return pl.pallas_call(
        _proj_kernel,
        grid=(t // tm,),
        in_specs=[
            pl.BlockSpec((tm, d), row),
            pl.BlockSpec((1, d), const),
            pl.BlockSpec((d, IN_WIDTH), const, pipeline_mode=pl.Buffered(1)),
            pl.BlockSpec((1, HEAD_DIM), const),
            pl.BlockSpec((1, HEAD_DIM), const),
        ] + [pl.BlockSpec((tm, HEAD_DIM), pos)] * 5,
        out_specs=[pl.BlockSpec((tm, w_), row) for w_ in widths],
        out_shape=[jax.ShapeDtypeStruct((t, w_), BF16) for w_ in widths],
        compiler_params=_params("parallel"),
        name="proj",
    )(h, g, w, gq, gk, *tables)


def _band_kernel(*refs, W, G, HKV, TQ, L, has_sink, want_lse):
    refs = list(refs)
    main_ref, prev_ref, next_ref = refs[:3]
    refs = refs[3:]
    sink_ref = refs.pop(0) if has_sink else None
    o_ref = refs.pop(0)
    lse_ref = refs.pop(0) if want_lse else None

    i = pl.program_id(2)
    QB = min(HEAD_DIM, TQ)
    KW = QB + 2 * W
    nb = TQ // QB

    def head(ref, idx):
        return ref[0, :, idx * HEAD_DIM:(idx + 1) * HEAD_DIM]

    qs, ks, vs = [], [], []
    for h in range(HKV):
        kh, vh = HKV * G + h, HKV * G + HKV + h
        kcat = jnp.concatenate([head(prev_ref, kh), head(main_ref, kh), head(next_ref, kh)], axis=0)
        vcat = jnp.concatenate([head(prev_ref, vh), head(main_ref, vh), head(next_ref, vh)], axis=0)
        for b in range(nb):
            qg = [main_ref[0, b * QB:(b + 1) * QB, (h * G + g) * HEAD_DIM:(h * G + g + 1) * HEAD_DIM]
                  for g in range(G)]
            qs.append(jnp.concatenate(qg, axis=0) if G > 1 else qg[0])
            ks.append(kcat[b * QB:b * QB + KW])
            vs.append(vcat[b * QB:b * QB + KW])
    q3, k3, v3 = jnp.stack(qs), jnp.stack(ks), jnp.stack(vs)

    shape = (HKV * nb, G * QB, KW)
    blk = lax.broadcasted_iota(jnp.int32, shape, 0) % nb
    qrow = lax.broadcasted_iota(jnp.int32, shape, 1) % QB
    kcol = lax.broadcasted_iota(jnp.int32, shape, 2)
    kpos = kcol + blk * QB + (i * TQ - W)
    valid = jnp.logical_and(jnp.abs(kcol - W - qrow) <= W, jnp.logical_and(kpos >= 0, kpos < L))

    s = jnp.einsum("bqd,bkd->bqk", q3, k3, preferred_element_type=F32)
    s = jnp.where(valid, s, MASK_VALUE)
    m = jnp.max(s, axis=-1, keepdims=True)
    p = jnp.exp(s - m)
    l = jnp.sum(p, axis=-1, keepdims=True)
    if has_sink:
        sink = jnp.concatenate([
            jnp.concatenate([jnp.full((nb, QB, 1), sink_ref[h * G + g], F32) for g in range(G)], axis=1)
            for h in range(HKV)], axis=0)
        l = l + jnp.exp(sink - m)
    o = jnp.einsum("bqk,bkd->bqd", p.astype(BF16), v3, preferred_element_type=F32) / l
    if want_lse:
        lse = jnp.broadcast_to(m + jnp.log(l), o.shape)
    for h in range(HKV):
        for b in range(nb):
            for g in range(G):
                col = (h * G + g) * HEAD_DIM
                dst = (0, slice(b * QB, (b + 1) * QB), slice(col, col + HEAD_DIM))
                o_ref[dst] = o[h * nb + b, g * QB:(g + 1) * QB].astype(o_ref.dtype)
                if want_lse:
                    lse_ref[dst] = lse[h * nb + b, g * QB:(g + 1) * QB]


def _band(x, sink, *, W, G, HKV, R, want_lse):
    bsz, L, _ = x.shape
    cw = (HKV * G + 2 * HKV) * HEAD_DIM
    ow = HKV * G * HEAD_DIM
    tq = min(BAND_ROWS, L)
    nblk, per = L // tq, tq // W
    has_sink = sink is not None
    in_specs = [
        pl.BlockSpec((1, tq, cw), lambda b, r, i: (b, i, r)),
        pl.BlockSpec((1, W, cw), lambda b, r, i: (b, jnp.maximum(i * per - 1, 0), r)),
        pl.BlockSpec((1, W, cw), lambda b, r, i: (b, jnp.minimum((i + 1) * per, L // W - 1), r)),
    ]
    args = [x, x, x]
    if has_sink:
        in_specs.append(pl.BlockSpec(memory_space=pltpu.SMEM))
        args.append(sink)
    out_spec = pl.BlockSpec((1, tq, ow), lambda b, r, i: (b, i, r))
    out_specs, out_shape = [out_spec], [jax.ShapeDtypeStruct((bsz, L, R * ow), BF16)]
    if want_lse:
        out_specs.append(out_spec)
        out_shape.append(jax.ShapeDtypeStruct((bsz, L, R * ow), F32))
    return pl.pallas_call(
        functools.partial(_band_kernel, W=W, G=G, HKV=HKV, TQ=tq, L=L, has_sink=has_sink, want_lse=want_lse),
        grid=(bsz, R, nblk),
        in_specs=in_specs,
        out_specs=out_specs,
        out_shape=out_shape,
        compiler_params=_params("parallel", "parallel", "parallel"),
        name="band_sink" if has_sink else f"band_dil{R}",
    )(*args)


ONES_ROWS = 16


def _flash_kernel(q_ref, k_ref, v_ref, o_ref, qs_ref, vt_ref, s_ref, m_ref, acc_ref, *, TQ, TK, G, S):
    nchunks = S // TK

    def rows(chunk):
        return pl.ds(pl.multiple_of(chunk * TK, TK), TK)

    @pl.when(pl.program_id(2) == 0)
    def _():
        vt_ref[HEAD_DIM:, :] = jnp.ones((ONES_ROWS, S), BF16)

        def fill(c, carry):
            vt_ref[:HEAD_DIM, rows(c)] = v_ref[0, rows(c), :].astype(F32).T.astype(BF16)
            return carry

        lax.fori_loop(0, nchunks, fill, 0)

    for g in range(G):
        qs_ref[g * TQ:(g + 1) * TQ, :] = q_ref[0, :, g * HEAD_DIM:(g + 1) * HEAD_DIM]
    m_ref[...] = jnp.full(m_ref.shape, MASK_VALUE, F32)
    acc_ref[...] = jnp.zeros(acc_ref.shape, F32)

    def scores(chunk, slot):
        s_ref[slot] = lax.dot_general(k_ref[0, rows(chunk), :], qs_ref[...], (((1,), (1,)), ((), ())),
                                      preferred_element_type=F32)

    def accumulate(chunk, slot):
        s = s_ref[slot]
        m_prev = m_ref[...]
        m_next = jnp.maximum(m_prev, jnp.max(s, axis=0, keepdims=True))
        alpha = jnp.exp2(m_prev - m_next)
        p = jnp.exp2(s - m_next).astype(BF16)
        acc_ref[...] = alpha * acc_ref[...] + jnp.dot(vt_ref[:, rows(chunk)], p, preferred_element_type=F32)
        m_ref[...] = m_next

    scores(0, 0)

    def step(j, carry):
        for u in range(FLASH_UNROLL):
            c = FLASH_UNROLL * j + u
            nxt = c + 1 if u < FLASH_UNROLL - 1 else jnp.minimum(c + 1, nchunks - 1)
            scores(nxt, (u + 1) % 2)
            accumulate(c, u % 2)
        return carry

    lax.fori_loop(0, nchunks // FLASH_UNROLL, step, 0)
    out = (acc_ref[:HEAD_DIM, :] / acc_ref[HEAD_DIM:HEAD_DIM + 1, :]).T
    for g in range(G):
        o_ref[0, :, g * HEAD_DIM:(g + 1) * HEAD_DIM] = out[g * TQ:(g + 1) * TQ].astype(BF16)


def _flash(q, kv):
    bsz, S, _ = q.shape
    G = C_Q_HEADS // C_KV_HEADS
    tq = min(FLASH_Q_ROWS, S)
    tk = min(FLASH_K_ROWS, S // FLASH_UNROLL)
    assert S % (tk * FLASH_UNROLL) == 0 and FLASH_UNROLL % 2 == 0
    gw = G * HEAD_DIM
    return pl.pallas_call(
        functools.partial(_flash_kernel, TQ=tq, TK=tk, G=G, S=S),
        grid=(bsz, C_KV_HEADS, S // tq),
        in_specs=[
            pl.BlockSpec((1, tq, gw), lambda b, h, i: (b, i, h)),
            pl.BlockSpec((1, S, HEAD_DIM), lambda b, h, i: (b, 0, h)),
            pl.BlockSpec((1, S, HEAD_DIM), lambda b, h, i: (b, 0, C_KV_HEADS + h)),
        ],
        out_specs=pl.BlockSpec((1, tq, gw), lambda b, h, i: (b, i, h)),
        out_shape=jax.ShapeDtypeStruct((bsz, S, C_Q), BF16),
        scratch_shapes=[
            pltpu.VMEM((G * tq, HEAD_DIM), BF16),
            pltpu.VMEM((HEAD_DIM + ONES_ROWS, S), BF16),
            pltpu.VMEM((2, tk, G * tq), F32),
            pltpu.VMEM((1, G * tq), F32),
            pltpu.VMEM((HEAD_DIM + ONES_ROWS, G * tq), F32),
        ],
        compiler_params=_params("parallel", "parallel", "arbitrary"),
        name="flash",
    )(q, kv, kv)


def _mix_kernel(h_ref, g_ref, ya_ref, ob0_ref, ob1_ref, ob2_ref, ls0_ref, ls1_ref, ls2_ref, yc_ref,
                wg0_ref, wg1_ref, wg2_ref, bg0_ref, bg1_ref, bg2_ref, wba_ref, wbb_ref, wbc_ref, wo_ref,
                o_ref, u_ref, yb_ref):
    c = pl.program_id(1)

    @pl.when(c == 0)
    def _():
        x = h_ref[...]
        u_ref[...] = _rms(x, g_ref[...]).astype(BF16)
        o_ref[...] = x
        obs = (ob0_ref, ob1_ref, ob2_ref)
        lss = (ls0_ref, ls1_ref, ls2_ref)
        gw = B_HEADS_PER_GROUP * HEAD_DIM
        for j in range(B_HEADS_PER_GROUP):
            sl = slice(j * HEAD_DIM, (j + 1) * HEAD_DIM)
            ls = [r[:, sl] for r in lss]
            m = jnp.maximum(jnp.maximum(ls[0], ls[1]), ls[2])
            es = [jnp.exp(v - m) for v in ls]
            den = es[0] + es[1] + es[2]
            for gi in range(len(B_GROUPS)):
                wt = es[gi] / den
                yb_ref[:, gi * gw + j * HEAD_DIM:gi * gw + (j + 1) * HEAD_DIM] = (
                    obs[gi][:, sl].astype(F32) * wt).astype(BF16)

    u = u_ref[...]

    def gate(w_ref, b_ref):
        return jax.nn.sigmoid(jnp.dot(u, w_ref[...], preferred_element_type=F32) + b_ref[...])

    merged = gate(wg0_ref, bg0_ref) * jnp.dot(ya_ref[...], wba_ref[...], preferred_element_type=F32)
    merged += gate(wg1_ref, bg1_ref) * jnp.dot(yb_ref[...], wbb_ref[...], preferred_element_type=F32)
    merged += gate(wg2_ref, bg2_ref) * jnp.dot(yc_ref[...], wbc_ref[...], preferred_element_type=F32)
    o_ref[...] += jnp.dot(merged.astype(BF16), wo_ref[...], preferred_element_type=F32)


def _mix(h, g, ya, obs, lss, yc, w_gate, b_gate, wb_a, wb_b, wb_c, w_o):
    t, d = h.shape
    tm = min(TOKEN_BLOCK, t)
    nc = d // MIX_CHUNK
    row = lambda i, c: (i, 0)
    gw = B_HEADS_PER_GROUP * HEAD_DIM
    gate_spec = lambda k: pl.BlockSpec((d, MIX_CHUNK), lambda i, c: (0, k * nc + c))
    bias_spec = lambda k: pl.BlockSpec((1, MIX_CHUNK), lambda i, c: (0, k * nc + c))
    col_spec = lambda rows: pl.BlockSpec((rows, MIX_CHUNK), lambda i, c: (0, c))
    return pl.pallas_call(
        _mix_kernel,
        grid=(t // tm, nc),
        in_specs=[
            pl.BlockSpec((tm, d), row),
            pl.BlockSpec((1, d), lambda i, c: (0, 0)),
            pl.BlockSpec((tm, A_Q), row),
            pl.BlockSpec((tm, gw), row), pl.BlockSpec((tm, gw), row), pl.BlockSpec((tm, gw), row),
            pl.BlockSpec((tm, gw), row), pl.BlockSpec((tm, gw), row), pl.BlockSpec((tm, gw), row),
            pl.BlockSpec((tm, C_Q), row),
            gate_spec(0), gate_spec(1), gate_spec(2),
            bias_spec(0), bias_spec(1), bias_spec(2),
            col_spec(A_Q), col_spec(B_W), col_spec(C_Q),
            pl.BlockSpec((MIX_CHUNK, d), lambda i, c: (c, 0)),
        ],
        out_specs=pl.BlockSpec((tm, d), row),
        out_shape=jax.ShapeDtypeStruct((t, d), F32),
        scratch_shapes=[pltpu.VMEM((tm, d), BF16), pltpu.VMEM((tm, B_W), BF16)],
        compiler_params=_params("parallel", "arbitrary"),
        name="mix",
    )(h, g, ya, *obs, *lss, yc, w_gate, w_gate, w_gate, b_gate, b_gate, b_gate, wb_a, wb_b, wb_c, w_o)


def _rope_tables(seq):
    def angles(pos, dim):
        inv = ROPE_THETA ** (-jnp.arange(0, dim, 2, dtype=F32) / dim)
        return pos.astype(F32)[:, None] * inv[None, :]

    pos = jnp.arange(seq)
    ang = angles(pos, HEAD_DIM)
    cos, sin = jnp.cos(ang), jnp.sin(ang)
    ar = angles(pos // GRID_W, HEAD_DIM // 2)
    ac = angles(pos % GRID_W, HEAD_DIM // 2)
    cr, sr, cc, sc = jnp.cos(ar), jnp.sin(ar), jnp.cos(ac), jnp.sin(ac)
    z = jnp.zeros_like(sr)
    return (
        jnp.concatenate([cos, cos], -1),
        jnp.concatenate([-sin, sin], -1),
        jnp.concatenate([cr, cr, cc, cc], -1),
        jnp.concatenate([-sr, z, -sc, z], -1),
        jnp.concatenate([z, sr, z, sc], -1),
    )


def _layer_weights(l, g_ffn1, w_ffn1_in, w_ffn1_out, g_mix, w_in, a_sink, c_q_norm, c_k_norm,
                   w_branch, w_gate, b_gate, w_o, g_ffn2, w_ffn2_in, w_ffn2_out):
    bf = lambda w: w.astype(BF16)
    vec = lambda v: v.reshape(1, -1)
    return dict(
        g_ffn1=vec(g_ffn1[l]), w_ffn1_in=bf(w_ffn1_in[l]), w_ffn1_out=bf(w_ffn1_out[l]),
        g_mix=vec(g_mix[l]), w_in=bf(w_in[l]), a_sink=a_sink[l].reshape(-1),
        c_q_norm=vec(c_q_norm[l]), c_k_norm=vec(c_k_norm[l]),
        wb_a=bf(w_branch[l, :A_Q]), wb_b=bf(w_branch[l, A_Q:A_Q + B_W]), wb_c=bf(w_branch[l, A_Q + B_W:]),
        w_gate=bf(w_gate[l]), b_gate=vec(b_gate[l]), w_o=bf(w_o[l]),
        g_ffn2=vec(g_ffn2[l]), w_ffn2_in=bf(w_ffn2_in[l]), w_ffn2_out=bf(w_ffn2_out[l]),
    )


def _trunk(x, layers, g_final):
    bsz, seq, d = x.shape
    t = bsz * seq
    tables = _rope_tables(seq)
    gw = B_HEADS_PER_GROUP * HEAD_DIM
    x = x.reshape(t, d)
    for li, w in enumerate(layers):
        h = _ffn(x, w["g_ffn1"], w["w_ffn1_in"], w["w_ffn1_out"], g_final, final_norm=False)
        a, b0, b1, b2, cq, ckv = _proj(h, w["g_mix"], w["w_in"], w["c_q_norm"], w["c_k_norm"], tables, seq)
        (ya,) = _band(a.reshape(bsz, seq, A_WIDTH), w["a_sink"], W=A_HALF_WINDOW,
                      G=A_Q_HEADS // A_KV_HEADS, HKV=A_KV_HEADS, R=1, want_lse=False)
        obs, lss = [], []
        for bg, (window, dil) in zip((b0, b1, b2), B_GROUPS):
            o, ls = _band(bg.reshape(bsz, seq // dil, dil * B_GROUP_WIDTH), None, W=window // (2 * dil),
                          G=1, HKV=B_HEADS_PER_GROUP, R=dil, want_lse=True)
            obs.append(o.reshape(t, gw))
            lss.append(ls.reshape(t, gw))
        yc = _flash(cq.reshape(bsz, seq, C_Q), ckv.reshape(bsz, seq, 2 * C_KV))
        h = _mix(h, w["g_mix"], ya.reshape(t, A_Q), obs, lss, yc.reshape(t, C_Q),
                 w["w_gate"], w["b_gate"], w["wb_a"], w["wb_b"], w["wb_c"], w["w_o"])
        x = _ffn(h, w["g_ffn2"], w["w_ffn2_in"], w["w_ffn2_out"], g_final,
                 final_norm=(li == len(layers) - 1))
    return x.reshape(bsz, seq, d)


def kernel(x_prompt, x_sample, g_ffn1, w_ffn1_in, w_ffn1_out, g_mix, w_in, a_sink, c_q_norm, c_k_norm,
           w_branch, w_gate, b_gate, w_o, g_ffn2, w_ffn2_in, w_ffn2_out, g_final):
    depth = g_ffn1.shape[0]
    layers = [
        _layer_weights(l, g_ffn1, w_ffn1_in, w_ffn1_out, g_mix, w_in, a_sink, c_q_norm, c_k_norm,
                       w_branch, w_gate, b_gate, w_o, g_ffn2, w_ffn2_in, w_ffn2_out)
        for l in range(depth)
    ]
    gf = g_final.reshape(1, -1)
    return _trunk(x_prompt, layers, gf), _trunk(x_sample, layers, gf)
```

```python
import functools

import jax
import jax.numpy as jnp
from jax import lax
from jax.experimental import pallas as pl
from jax.experimental.pallas import tpu as pltpu

F32 = jnp.float32
BF16 = jnp.bfloat16

D_MODEL = 2048
D_FF = 5632
HEAD_DIM = 128
A_Q_HEADS, A_KV_HEADS, A_HALF_WINDOW = 4, 2, 128
B_GROUPS = ((128, 1), (512, 4), (2048, 16))
B_HEADS_PER_GROUP = 2
C_Q_HEADS, C_KV_HEADS = 6, 2
GRID_W = 64
ROPE_THETA = 10000.0
N_BRANCH = 3
NORM_EPS = 1e-6
MASK_VALUE = -1e30
Q_SCALE = HEAD_DIM ** -0.5
LOG2_E = 1.4426950408889634

A_Q = A_Q_HEADS * HEAD_DIM
A_KV = A_KV_HEADS * HEAD_DIM
B_W = B_HEADS_PER_GROUP * len(B_GROUPS) * HEAD_DIM
C_Q = C_Q_HEADS * HEAD_DIM
C_KV = C_KV_HEADS * HEAD_DIM
A_WIDTH = A_Q + 2 * A_KV
B_GROUP_WIDTH = 3 * B_HEADS_PER_GROUP * HEAD_DIM
IN_WIDTH = A_WIDTH + 3 * B_W + C_Q + 2 * C_KV

VMEM_LIMIT_BYTES = 58 * 1024 * 1024

TOKEN_BLOCK = 512
FFN_TOKEN_BLOCK = 1024
FF_CHUNK = 512
PROJ_CHUNK = 512
PROJ_STAGE_SLOTS = 4
MIX_CHUNK = 512
BAND_ROWS = 512
FLASH_Q_ROWS = 256
FLASH_K_ROWS = 512
FLASH_UNROLL = 8


def _params(*sem):
    return pltpu.CompilerParams(dimension_semantics=sem, vmem_limit_bytes=VMEM_LIMIT_BYTES)


def _rms(x, g):
    return x * lax.rsqrt(jnp.mean(x * x, axis=-1, keepdims=True) + NORM_EPS) * g


def _ffn_kernel(x_ref, g_ref, wg_ref, wu_ref, wo_ref, gf_ref, o_ref, n_ref, *, final_norm):
    j = pl.program_id(1)

    @pl.when(j == 0)
    def _():
        x = x_ref[...]
        n_ref[...] = _rms(x, g_ref[...]).astype(BF16)
        o_ref[...] = x

    n = n_ref[...]
    gate = jnp.dot(n, wg_ref[...], preferred_element_type=F32)
    up = jnp.dot(n, wu_ref[...], preferred_element_type=F32)
    a = (gate * jax.nn.sigmoid(gate) * (0.5 * up)).astype(BF16)
    o_ref[...] += jnp.dot(a, wo_ref[...], preferred_element_type=F32)

    if final_norm:
        @pl.when(j == pl.num_programs(1) - 1)
        def _():
            o_ref[...] = _rms(o_ref[...], gf_ref[...])


def _ffn(x, g, w_in, w_out, g_final, *, layer, final_norm):
    t, d = x.shape
    tm = min(FFN_TOKEN_BLOCK, t)
    nf = D_FF // FF_CHUNK
    return pl.pallas_call(
        functools.partial(_ffn_kernel, final_norm=final_norm),
        grid=(t // tm, nf),
        in_specs=[
            pl.BlockSpec((tm, d), lambda i, j: (i, 0)),
            pl.BlockSpec((1, d), lambda i, j: (0, 0)),
            pl.BlockSpec((None, d, FF_CHUNK), lambda i, j: (layer, 0, j)),
            pl.BlockSpec((None, d, FF_CHUNK), lambda i, j: (layer, 0, j + nf)),
            pl.BlockSpec((None, FF_CHUNK, d), lambda i, j: (layer, j, 0)),
            pl.BlockSpec((1, d), lambda i, j: (0, 0)),
        ],
        out_specs=pl.BlockSpec((tm, d), lambda i, j: (i, 0)),
        out_shape=jax.ShapeDtypeStruct((t, d), F32),
        scratch_shapes=[pltpu.VMEM((tm, d), BF16)],
        compiler_params=_params("parallel", "arbitrary"),
        name="ffn_final" if final_norm else "ffn",
    )(x, g, w_in, w_in, w_out, g_final)


def _rope(y, cos, sin):
    return y * cos + pltpu.roll(y, HEAD_DIM // 2, 1) * sin


def _axial_rope(y, cos, sin_lo, sin_hi):
    q = HEAD_DIM // 4
    return y * cos + pltpu.roll(y, HEAD_DIM - q, 1) * sin_lo + pltpu.roll(y, q, 1) * sin_hi


def _proj_layout():
    heads = []
    hpg = B_HEADS_PER_GROUP
    for h in range(A_Q_HEADS):
        heads.append(("rope_q", 0, h * HEAD_DIM))
    for h in range(A_KV_HEADS):
        heads.append(("rope", 0, A_Q + h * HEAD_DIM))
    for h in range(A_KV_HEADS):
        heads.append(("plain", 0, A_Q + A_KV + h * HEAD_DIM))
    for part, kind in enumerate(("rope_q", "rope", "plain")):
        for h in range(hpg * len(B_GROUPS)):
            heads.append((kind, 1 + h // hpg, (part * hpg + h % hpg) * HEAD_DIM))
    for h in range(C_Q_HEADS):
        heads.append(("axial_q", 4, h * HEAD_DIM))
    for h in range(C_KV_HEADS):
        heads.append(("axial_k", 5, h * HEAD_DIM))
    for h in range(C_KV_HEADS):
        heads.append(("plain", 5, C_KV + h * HEAD_DIM))
    assert len(heads) * HEAD_DIM == IN_WIDTH
    return heads


def _proj_kernel(h_ref, g_ref, w_ref, gq_ref, gk_ref, cos_ref, sin_ref, cosc_ref, sinlo_ref, sinhi_ref,
                 a_ref, b0_ref, b1_ref, b2_ref, cq_ref, ckv_ref, stage_ref):
    outs = (a_ref, b0_ref, b1_ref, b2_ref, cq_ref, ckv_ref)
    u = _rms(h_ref[...], g_ref[...]).astype(BF16)
    layout = _proj_layout()
    heads_per_chunk = PROJ_CHUNK // HEAD_DIM
    staged = 0
    nchunk = IN_WIDTH // PROJ_CHUNK
    cost = {"plain": 0, "rope": 1, "rope_q": 1, "axial_q": 2, "axial_k": 2}
    order = sorted(range(nchunk), key=lambda c: -sum(
        cost[layout[c * heads_per_chunk + hh][0]] for hh in range(heads_per_chunk)))
    for c in order:
        p = jnp.dot(u, w_ref[:, c * PROJ_CHUNK:(c + 1) * PROJ_CHUNK], preferred_element_type=F32)
        for hh in range(heads_per_chunk):
            kind, oi, col = layout[c * heads_per_chunk + hh]
            y = p[:, hh * HEAD_DIM:(hh + 1) * HEAD_DIM]
            if kind in ("rope", "rope_q"):
                y = _rope(y, cos_ref[...], sin_ref[...])
            elif kind in ("axial_q", "axial_k"):
                y = _rms(y, gq_ref[...] if kind == "axial_q" else gk_ref[...])
                y = _axial_rope(y, cosc_ref[...], sinlo_ref[...], sinhi_ref[...])
            if kind == "rope_q":
                y = y * Q_SCALE
            elif kind == "axial_q":
                y = y * (Q_SCALE * LOG2_E)
            dil = B_GROUPS[oi - 1][1] if 1 <= oi <= len(B_GROUPS) else 1
            if dil == 1:
                outs[oi][:, col:col + HEAD_DIM] = y.astype(BF16)
            else:
                slot = staged % PROJ_STAGE_SLOTS
                staged += 1
                stage_ref[slot] = y
                n = y.shape[0] // dil
                for r in range(dil):
                    outs[oi][r, :, col:col + HEAD_DIM] = stage_ref[slot, pl.ds(r, n, stride=dil), :].astype(BF16)


def _proj(h, g, w, gq, gk, tables, bsz, seq, layer):
    t, d = h.shape
    tm = min(TOKEN_BLOCK, seq)
    nseq = seq // tm
    row = lambda i: (i, 0)
    const = lambda i: (0, 0)
    pos = lambda i: (i % nseq, 0)
    out_specs, out_shape = [], []
    for width, dil in ((A_WIDTH, 1),) + tuple((B_GROUP_WIDTH, dl) for _, dl in B_GROUPS) + ((C_Q, 1), (2 * C_KV, 1)):
        if dil == 1:
            out_specs.append(pl.BlockSpec((tm, width), row))
            out_shape.append(jax.ShapeDtypeStruct((t, width), BF16))
        else:
            out_specs.append(pl.BlockSpec((None, dil, tm // dil, width), lambda i: (i // nseq, 0, i % nseq, 0)))
            out_shape.append(jax.ShapeDtypeStruct((bsz, dil, seq // dil, width), BF16))
    return pl.pallas_call(
        _proj_kernel,
        grid=(t // tm,),
        in_specs=[
            pl.BlockSpec((tm, d), row),
            pl.BlockSpec((1, d), const),
            pl.BlockSpec((None, d, IN_WIDTH), lambda i: (layer, 0, 0), pipeline_mode=pl.Buffered(1)),
            pl.BlockSpec((1, HEAD_DIM), const),
            pl.BlockSpec((1, HEAD_DIM), const),
        ] + [pl.BlockSpec((tm, HEAD_DIM), pos)] * 5,
        out_specs=out_specs,
        out_shape=out_shape,
        scratch_shapes=[pltpu.VMEM((PROJ_STAGE_SLOTS, tm, HEAD_DIM), F32)],
        compiler_params=_params("parallel"),
        name="proj",
    )(h, g, w, gq, gk, *tables)


def _band_kernel(*refs, W, G, HKV, TQ, L, has_sink, want_lse):
    refs = list(refs)
    main_ref, prev_ref, next_ref = refs[:3]
    refs = refs[3:]
    sink_ref = refs.pop(0) if has_sink else None
    o_ref = refs.pop(0)
    lse_ref = refs.pop(0) if want_lse else None

    i = pl.program_id(2)
    QB = min(HEAD_DIM, TQ)
    KW = QB + 2 * W
    nb = TQ // QB

    def head(ref, idx):
        return ref[:, idx * HEAD_DIM:(idx + 1) * HEAD_DIM]

    qs, ks, vs = [], [], []
    for h in range(HKV):
        kh, vh = HKV * G + h, HKV * G + HKV + h
        kcat = jnp.concatenate([head(prev_ref, kh), head(main_ref, kh), head(next_ref, kh)], axis=0)
        vcat = jnp.concatenate([head(prev_ref, vh), head(main_ref, vh), head(next_ref, vh)], axis=0)
        for b in range(nb):
            qg = [main_ref[b * QB:(b + 1) * QB, (h * G + g) * HEAD_DIM:(h * G + g + 1) * HEAD_DIM]
                  for g in range(G)]
            qs.append(jnp.concatenate(qg, axis=0) if G > 1 else qg[0])
            ks.append(kcat[b * QB:b * QB + KW])
            vs.append(vcat[b * QB:b * QB + KW])
    q3, k3, v3 = jnp.stack(qs), jnp.stack(ks), jnp.stack(vs)

    shape = (HKV * nb, G * QB, KW)
    blk = lax.broadcasted_iota(jnp.int32, shape, 0) % nb
    qrow = lax.broadcasted_iota(jnp.int32, shape, 1) % QB
    kcol = lax.broadcasted_iota(jnp.int32, shape, 2)
    kpos = kcol + blk * QB + (i * TQ - W)
    valid = jnp.logical_and(jnp.abs(kcol - W - qrow) <= W, jnp.logical_and(kpos >= 0, kpos < L))

    s = jnp.einsum("bqd,bkd->bqk", q3, k3, preferred_element_type=F32)
    s = jnp.where(valid, s, MASK_VALUE)
    m = jnp.max(s, axis=-1, keepdims=True)
    p = jnp.exp(s - m)
    l = jnp.sum(p, axis=-1, keepdims=True)
    if has_sink:
        sink = jnp.concatenate([
            jnp.concatenate([jnp.full((nb, QB, 1), sink_ref[h * G + g], F32) for g in range(G)], axis=1)
            for h in range(HKV)], axis=0)
        l = l + jnp.exp(sink - m)
    o = jnp.einsum("bqk,bkd->bqd", p.astype(BF16), v3, preferred_element_type=F32) / l
    if want_lse:
        lse = jnp.broadcast_to(m + jnp.log(l), o.shape)
    for h in range(HKV):
        for b in range(nb):
            for g in range(G):
                col = (h * G + g) * HEAD_DIM
                dst = (slice(b * QB, (b + 1) * QB), slice(col, col + HEAD_DIM))
                o_ref[dst] = o[h * nb + b, g * QB:(g + 1) * QB].astype(o_ref.dtype)
                if want_lse:
                    lse_ref[dst] = lse[h * nb + b, g * QB:(g + 1) * QB]


def _band(x, sink, *, W, G, HKV, want_lse):
    bsz, R, L, cw = x.shape
    assert cw == (HKV * G + 2 * HKV) * HEAD_DIM
    ow = HKV * G * HEAD_DIM
    tq = min(BAND_ROWS, L)
    nblk, per = L // tq, tq // W
    has_sink = sink is not None
    in_specs = [
        pl.BlockSpec((None, None, tq, cw), lambda b, r, i: (b, r, i, 0)),
        pl.BlockSpec((None, None, W, cw), lambda b, r, i: (b, r, jnp.maximum(i * per - 1, 0), 0)),
        pl.BlockSpec((None, None, W, cw), lambda b, r, i: (b, r, jnp.minimum((i + 1) * per, L // W - 1), 0)),
    ]
    args = [x, x, x]
    if has_sink:
        in_specs.append(pl.BlockSpec(memory_space=pltpu.SMEM))
        args.append(sink)
    out_spec = pl.BlockSpec((None, None, tq, ow), lambda b, r, i: (b, r, i, 0))
    out_specs, out_shape = [out_spec], [jax.ShapeDtypeStruct((bsz, R, L, ow), BF16)]
    if want_lse:
        out_specs.append(out_spec)
        out_shape.append(jax.ShapeDtypeStruct((bsz, R, L, ow), F32))
    return pl.pallas_call(
        functools.partial(_band_kernel, W=W, G=G, HKV=HKV, TQ=tq, L=L, has_sink=has_sink, want_lse=want_lse),
        grid=(bsz, R, nblk),
        in_specs=in_specs,
        out_specs=out_specs,
        out_shape=out_shape,
        compiler_params=_params("parallel", "parallel", "parallel"),
        name="band_sink" if has_sink else f"band_dil{R}",
    )(*args)


ONES_ROWS = 16


def _flash_kernel(q_ref, k_ref, v_ref, o_ref, qs_ref, vt_ref, s_ref, m_ref, acc_ref, *, TQ, TK, G, S):
    nchunks = S // TK

    def rows(chunk):
        return pl.ds(pl.multiple_of(chunk * TK, TK), TK)

    @pl.when(pl.program_id(2) == 0)
    def _():
        vt_ref[HEAD_DIM:, :] = jnp.ones((ONES_ROWS, S), BF16)

        def fill(c, carry):
            vt_ref[:HEAD_DIM, rows(c)] = v_ref[0, rows(c), :].astype(F32).T.astype(BF16)
            return carry

        lax.fori_loop(0, nchunks, fill, 0)

    for g in range(G):
        qs_ref[g * TQ:(g + 1) * TQ, :] = q_ref[0, :, g * HEAD_DIM:(g + 1) * HEAD_DIM]
    m_ref[...] = jnp.full(m_ref.shape, MASK_VALUE, F32)
    acc_ref[...] = jnp.zeros(acc_ref.shape, F32)

    def scores(chunk, slot):
        s_ref[slot] = lax.dot_general(k_ref[0, rows(chunk), :], qs_ref[...], (((1,), (1,)), ((), ())),
                                      preferred_element_type=F32)

    def accumulate(chunk, slot):
        s = s_ref[slot]
        m_prev = m_ref[...]
        m_next = jnp.maximum(m_prev, jnp.max(s, axis=0, keepdims=True))
        alpha = jnp.exp2(m_prev - m_next)
        p = jnp.exp2(s - m_next).astype(BF16)
        acc_ref[...] = alpha * acc_ref[...] + jnp.dot(vt_ref[:, rows(chunk)], p, preferred_element_type=F32)
        m_ref[...] = m_next

    scores(0, 0)

    def step(j, carry):
        for u in range(FLASH_UNROLL):
            c = FLASH_UNROLL * j + u
            nxt = c + 1 if u < FLASH_UNROLL - 1 else jnp.minimum(c + 1, nchunks - 1)
            scores(nxt, (u + 1) % 2)
            accumulate(c, u % 2)
        return carry

    lax.fori_loop(0, nchunks // FLASH_UNROLL, step, 0)
    out = (acc_ref[:HEAD_DIM, :] / acc_ref[HEAD_DIM:HEAD_DIM + 1, :]).T
    for g in range(G):
        o_ref[0, :, g * HEAD_DIM:(g + 1) * HEAD_DIM] = out[g * TQ:(g + 1) * TQ].astype(BF16)


def _flash(q, kv):
    bsz, S, _ = q.shape
    G = C_Q_HEADS // C_KV_HEADS
    tq = min(FLASH_Q_ROWS, S)
    tk = min(FLASH_K_ROWS, S // FLASH_UNROLL)
    assert S % (tk * FLASH_UNROLL) == 0 and FLASH_UNROLL % 2 == 0
    gw = G * HEAD_DIM
    return pl.pallas_call(
        functools.partial(_flash_kernel, TQ=tq, TK=tk, G=G, S=S),
        grid=(bsz, C_KV_HEADS, S // tq),
        in_specs=[
            pl.BlockSpec((1, tq, gw), lambda b, h, i: (b, i, h)),
            pl.BlockSpec((1, S, HEAD_DIM), lambda b, h, i: (b, 0, h)),
            pl.BlockSpec((1, S, HEAD_DIM), lambda b, h, i: (b, 0, C_KV_HEADS + h)),
        ],
        out_specs=pl.BlockSpec((1, tq, gw), lambda b, h, i: (b, i, h)),
        out_shape=jax.ShapeDtypeStruct((bsz, S, C_Q), BF16),
        scratch_shapes=[
            pltpu.VMEM((G * tq, HEAD_DIM), BF16),
            pltpu.VMEM((HEAD_DIM + ONES_ROWS, S), BF16),
            pltpu.VMEM((2, tk, G * tq), F32),
            pltpu.VMEM((1, G * tq), F32),
            pltpu.VMEM((HEAD_DIM + ONES_ROWS, G * tq), F32),
        ],
        compiler_params=_params("parallel", "parallel", "arbitrary"),
        name="flash",
    )(q, kv, kv)


def _mix_kernel(h_ref, g_ref, ya_ref, ob0_ref, ob1_ref, ob2_ref, ls0_ref, ls1_ref, ls2_ref, yc_ref,
                wg0_ref, wg1_ref, wg2_ref, bg0_ref, bg1_ref, bg2_ref, wba_ref, wbb_ref, wbc_ref, wo_ref,
                o_ref, u_ref, yb_ref, tok_ref):
    c = pl.program_id(1)

    @pl.when(c == 0)
    def _():
        x = h_ref[...]
        u_ref[...] = _rms(x, g_ref[...]).astype(BF16)
        o_ref[...] = x
        obs = (ob0_ref, ob1_ref, ob2_ref)
        lss = (ls0_ref, ls1_ref, ls2_ref)
        gw = B_HEADS_PER_GROUP * HEAD_DIM
        slot = 0

        def token_major(ref, sl, dil, slot):
            if dil == 1:
                return ref[:, sl].astype(F32)
            n = ref.shape[1]
            for r in range(dil):
                tok_ref[slot, pl.ds(r, n, stride=dil), :] = ref[r, :, sl].astype(F32)
            return tok_ref[slot]

        for j in range(B_HEADS_PER_GROUP):
            sl = slice(j * HEAD_DIM, (j + 1) * HEAD_DIM)
            ls, ob = [], []
            for gi, (_, dil) in enumerate(B_GROUPS):
                ls.append(token_major(lss[gi], sl, dil, slot))
                ob.append(token_major(obs[gi], sl, dil, slot + 1))
                slot += 2 if dil > 1 else 0
            m = jnp.maximum(jnp.maximum(ls[0], ls[1]), ls[2])
            es = [jnp.exp(v - m) for v in ls]
            den = es[0] + es[1] + es[2]
            for gi in range(len(B_GROUPS)):
                wt = es[gi] / den
                yb_ref[:, gi * gw + j * HEAD_DIM:gi * gw + (j + 1) * HEAD_DIM] = (ob[gi] * wt).astype(BF16)

    u = u_ref[...]

    def gate(w_ref, b_ref):
        return jax.nn.sigmoid(jnp.dot(u, w_ref[...], preferred_element_type=F32) + b_ref[...])

    merged = gate(wg0_ref, bg0_ref) * jnp.dot(ya_ref[...], wba_ref[...], preferred_element_type=F32)
    merged += gate(wg1_ref, bg1_ref) * jnp.dot(yb_ref[...], wbb_ref[...], preferred_element_type=F32)
    merged += gate(wg2_ref, bg2_ref) * jnp.dot(yc_ref[...], wbc_ref[...], preferred_element_type=F32)
    o_ref[...] += jnp.dot(merged.astype(BF16), wo_ref[...], preferred_element_type=F32)


def _mix(h, g, ya, obs, lss, yc, w_gate, b_gate, wb_a, wb_b, wb_c, w_o, seq, layer):
    t, d = h.shape
    tm = min(TOKEN_BLOCK, seq)
    nseq = seq // tm
    nc = d // MIX_CHUNK
    row = lambda i, c: (i, 0)
    gw = B_HEADS_PER_GROUP * HEAD_DIM
    group_specs = [
        pl.BlockSpec((tm, gw), row) if dil == 1 else
        pl.BlockSpec((None, dil, tm // dil, gw), lambda i, c: (i // nseq, 0, i % nseq, 0))
        for _, dil in B_GROUPS]
    n_stage = 2 * B_HEADS_PER_GROUP * sum(dil > 1 for _, dil in B_GROUPS)
    gate_spec = lambda k: pl.BlockSpec((None, d, MIX_CHUNK), lambda i, c: (layer, 0, k * nc + c))
    bias_spec = lambda k: pl.BlockSpec((1, MIX_CHUNK), lambda i, c: (0, k * nc + c))
    col_spec = lambda rows: pl.BlockSpec((None, rows, MIX_CHUNK), lambda i, c: (layer, 0, c))
    return pl.pallas_call(
        _mix_kernel,
        grid=(t // tm, nc),
        in_specs=[
            pl.BlockSpec((tm, d), row),
            pl.BlockSpec((1, d), lambda i, c: (0, 0)),
            pl.BlockSpec((tm, A_Q), row),
            *group_specs, *group_specs,
            pl.BlockSpec((tm, C_Q), row),
            gate_spec(0), gate_spec(1), gate_spec(2),
            bias_spec(0), bias_spec(1), bias_spec(2),
            col_spec(A_Q), col_spec(B_W), col_spec(C_Q),
            pl.BlockSpec((None, MIX_CHUNK, d), lambda i, c: (layer, c, 0)),
        ],
        out_specs=pl.BlockSpec((tm, d), row),
        out_shape=jax.ShapeDtypeStruct((t, d), F32),
        scratch_shapes=[pltpu.VMEM((tm, d), BF16), pltpu.VMEM((tm, B_W), BF16),
                        pltpu.VMEM((n_stage, tm, HEAD_DIM), F32)],
        compiler_params=_params("parallel", "arbitrary"),
        name="mix",
    )(h, g, ya, *obs, *lss, yc, w_gate, w_gate, w_gate, b_gate, b_gate, b_gate, wb_a, wb_b, wb_c, w_o)


def _rope_tables(seq):
    def angles(pos, dim):
        inv = ROPE_THETA ** (-jnp.arange(0, dim, 2, dtype=F32) / dim)
        return pos.astype(F32)[:, None] * inv[None, :]

    pos = jnp.arange(seq)
    ang = angles(pos, HEAD_DIM)
    cos, sin = jnp.cos(ang), jnp.sin(ang)
    ar = angles(pos // GRID_W, HEAD_DIM // 2)
    ac = angles(pos % GRID_W, HEAD_DIM // 2)
    cr, sr, cc, sc = jnp.cos(ar), jnp.sin(ar), jnp.cos(ac), jnp.sin(ac)
    z = jnp.zeros_like(sr)
    return (
        jnp.concatenate([cos, cos], -1),
        jnp.concatenate([-sin, sin], -1),
        jnp.concatenate([cr, cr, cc, cc], -1),
        jnp.concatenate([-sr, z, -sc, z], -1),
        jnp.concatenate([z, sr, z, sc], -1),
    )


def _prepare_weights(g_ffn1, w_ffn1_in, w_ffn1_out, g_mix, w_in, a_sink, c_q_norm, c_k_norm,
                     w_branch, w_gate, b_gate, w_o, g_ffn2, w_ffn2_in, w_ffn2_out):
    bf = lambda w: w.astype(BF16)
    depth = g_ffn1.shape[0]
    rows = lambda v: [v[l].reshape(1, -1) for l in range(depth)]
    return dict(
        g_ffn1=rows(g_ffn1), w_ffn1_in=bf(w_ffn1_in), w_ffn1_out=bf(w_ffn1_out),
        g_mix=rows(g_mix), w_in=bf(w_in), a_sink=[a_sink[l].reshape(-1) for l in range(depth)],
        c_q_norm=rows(c_q_norm), c_k_norm=rows(c_k_norm),
        wb_a=bf(w_branch[:, :A_Q]), wb_b=bf(w_branch[:, A_Q:A_Q + B_W]), wb_c=bf(w_branch[:, A_Q + B_W:]),
        w_gate=bf(w_gate), b_gate=rows(b_gate), w_o=bf(w_o),
        g_ffn2=rows(g_ffn2), w_ffn2_in=bf(w_ffn2_in), w_ffn2_out=bf(w_ffn2_out),
    )


def _trunk(x, w, g_final):
    bsz, seq, d = x.shape
    t = bsz * seq
    depth = len(w["g_ffn1"])
    tables = _rope_tables(seq)
    gw = B_HEADS_PER_GROUP * HEAD_DIM
    x = x.reshape(t, d)
    for l in range(depth):
        h = _ffn(x, w["g_ffn1"][l], w["w_ffn1_in"], w["w_ffn1_out"], g_final, layer=l, final_norm=False)
        a, b0, b1, b2, cq, ckv = _proj(h, w["g_mix"][l], w["w_in"], w["c_q_norm"][l], w["c_k_norm"][l],
                                       tables, bsz, seq, l)
        (ya,) = _band(a.reshape(bsz, 1, seq, A_WIDTH), w["a_sink"][l], W=A_HALF_WINDOW,
                      G=A_Q_HEADS // A_KV_HEADS, HKV=A_KV_HEADS, want_lse=False)
        obs, lss = [], []
        for bg, (window, dil) in zip((b0, b1, b2), B_GROUPS):
            o, ls = _band(bg.reshape(bsz, dil, seq // dil, B_GROUP_WIDTH), None, W=window // (2 * dil),
                          G=1, HKV=B_HEADS_PER_GROUP, want_lse=True)
            obs.append(o.reshape(t, gw) if dil == 1 else o)
            lss.append(ls.reshape(t, gw) if dil == 1 else ls)
        yc = _flash(cq.reshape(bsz, seq, C_Q), ckv.reshape(bsz, seq, 2 * C_KV))
        h = _mix(h, w["g_mix"][l], ya.reshape(t, A_Q), obs, lss, yc.reshape(t, C_Q),
                 w["w_gate"], w["b_gate"][l], w["wb_a"], w["wb_b"], w["wb_c"], w["w_o"], seq, l)
        x = _ffn(h, w["g_ffn2"][l], w["w_ffn2_in"], w["w_ffn2_out"], g_final, layer=l,
                 final_norm=(l == depth - 1))
    return x.reshape(bsz, seq, d)


def kernel(x_prompt, x_sample, g_ffn1, w_ffn1_in, w_ffn1_out, g_mix, w_in, a_sink, c_q_norm, c_k_norm,
           w_branch, w_gate, b_gate, w_o, g_ffn2, w_ffn2_in, w_ffn2_out, g_final):
    w = _prepare_weights(g_ffn1, w_ffn1_in, w_ffn1_out, g_mix, w_in, a_sink, c_q_norm, c_k_norm,
                         w_branch, w_gate, b_gate, w_o, g_ffn2, w_ffn2_in, w_ffn2_out)
    gf = g_final.reshape(1, -1)
    return _trunk(x_prompt, w, gf), _trunk(x_sample, w, gf)
```

```python
import functools

import jax
import jax.numpy as jnp
from jax import lax
from jax.experimental import pallas as pl
from jax.experimental.pallas import tpu as pltpu

F32 = jnp.float32
BF16 = jnp.bfloat16

D_MODEL = 2048
D_FF = 5632
HEAD_DIM = 128
A_Q_HEADS, A_KV_HEADS, A_HALF_WINDOW = 4, 2, 128
B_GROUPS = ((128, 1), (512, 4), (2048, 16))
B_HEADS_PER_GROUP = 2
C_Q_HEADS, C_KV_HEADS = 6, 2
GRID_W = 64
ROPE_THETA = 10000.0
N_BRANCH = 3
NORM_EPS = 1e-6
MASK_VALUE = -1e30
Q_SCALE = HEAD_DIM ** -0.5
LOG2_E = 1.4426950408889634

A_Q = A_Q_HEADS * HEAD_DIM
A_KV = A_KV_HEADS * HEAD_DIM
B_W = B_HEADS_PER_GROUP * len(B_GROUPS) * HEAD_DIM
C_Q = C_Q_HEADS * HEAD_DIM
C_KV = C_KV_HEADS * HEAD_DIM
A_WIDTH = A_Q + 2 * A_KV
B_GROUP_WIDTH = 3 * B_HEADS_PER_GROUP * HEAD_DIM
IN_WIDTH = A_WIDTH + 3 * B_W + C_Q + 2 * C_KV

VMEM_LIMIT_BYTES = 58 * 1024 * 1024

TOKEN_BLOCK = 512
FFN_TOKEN_BLOCK = 1024
FF_CHUNK = 512
PROJ_CHUNK = 512
PROJ_STAGE_SLOTS = 4
MIX_CHUNK = 512
BAND_ROWS = 512
BAND_ROWS_DILATED = 2048
FLASH_Q_ROWS = 256
FLASH_K_ROWS = 512
FLASH_UNROLL = 8


def _params(*sem):
    return pltpu.CompilerParams(dimension_semantics=sem, vmem_limit_bytes=VMEM_LIMIT_BYTES)


def _rms(x, g):
    return x * lax.rsqrt(jnp.mean(x * x, axis=-1, keepdims=True) + NORM_EPS) * g


def _ffn_kernel(x_ref, g_ref, wg_ref, wu_ref, wo_ref, gf_ref, o_ref, n_ref, *, final_norm):
    j = pl.program_id(1)

    @pl.when(j == 0)
    def _():
        x = x_ref[...]
        n_ref[...] = _rms(x, g_ref[...]).astype(BF16)
        o_ref[...] = x

    n = n_ref[...]
    gate = jnp.dot(n, wg_ref[...], preferred_element_type=F32)
    up = jnp.dot(n, wu_ref[...], preferred_element_type=F32)
    a = (gate * jax.nn.sigmoid(gate) * (0.5 * up)).astype(BF16)
    o_ref[...] += jnp.dot(a, wo_ref[...], preferred_element_type=F32)

    if final_norm:
        @pl.when(j == pl.num_programs(1) - 1)
        def _():
            o_ref[...] = _rms(o_ref[...], gf_ref[...])


def _ffn(x, g, w_in, w_out, g_final, *, layer, final_norm):
    t, d = x.shape
    tm = min(FFN_TOKEN_BLOCK, t)
    nf = D_FF // FF_CHUNK
    return pl.pallas_call(
        functools.partial(_ffn_kernel, final_norm=final_norm),
        grid=(t // tm, nf),
        in_specs=[
            pl.BlockSpec((tm, d), lambda i, j: (i, 0)),
            pl.BlockSpec((1, d), lambda i, j: (0, 0)),
            pl.BlockSpec((None, d, FF_CHUNK), lambda i, j: (layer, 0, j)),
            pl.BlockSpec((None, d, FF_CHUNK), lambda i, j: (layer, 0, j + nf)),
            pl.BlockSpec((None, FF_CHUNK, d), lambda i, j: (layer, j, 0)),
            pl.BlockSpec((1, d), lambda i, j: (0, 0)),
        ],
        out_specs=pl.BlockSpec((tm, d), lambda i, j: (i, 0)),
        out_shape=jax.ShapeDtypeStruct((t, d), F32),
        scratch_shapes=[pltpu.VMEM((tm, d), BF16)],
        compiler_params=_params("parallel", "arbitrary"),
        name="ffn_final" if final_norm else "ffn",
    )(x, g, w_in, w_in, w_out, g_final)


def _rope(y, cos, sin):
    return y * cos + pltpu.roll(y, HEAD_DIM // 2, 1) * sin


def _axial_rope(y, cos, sin_lo, sin_hi):
    q = HEAD_DIM // 4
    return y * cos + pltpu.roll(y, HEAD_DIM - q, 1) * sin_lo + pltpu.roll(y, q, 1) * sin_hi


def _proj_layout():
    heads = []
    hpg = B_HEADS_PER_GROUP
    for h in range(A_Q_HEADS):
        heads.append(("rope_q", 0, h * HEAD_DIM))
    for h in range(A_KV_HEADS):
        heads.append(("rope", 0, A_Q + h * HEAD_DIM))
    for h in range(A_KV_HEADS):
        heads.append(("plain", 0, A_Q + A_KV + h * HEAD_DIM))
    for part, kind in enumerate(("rope_q", "rope", "plain")):
        for h in range(hpg * len(B_GROUPS)):
            heads.append((kind, 1 + h // hpg, (part * hpg + h % hpg) * HEAD_DIM))
    for h in range(C_Q_HEADS):
        heads.append(("axial_q", 4, h * HEAD_DIM))
    for h in range(C_KV_HEADS):
        heads.append(("axial_k", 5, h * HEAD_DIM))
    for h in range(C_KV_HEADS):
        heads.append(("plain", 5, C_KV + h * HEAD_DIM))
    assert len(heads) * HEAD_DIM == IN_WIDTH
    return heads


def _proj_kernel(h_ref, g_ref, w_ref, gq_ref, gk_ref, cos_ref, sin_ref, cosc_ref, sinlo_ref, sinhi_ref,
                 a_ref, b0_ref, b1_ref, b2_ref, cq_ref, ckv_ref, u_ref, stage_ref):
    outs = (a_ref, b0_ref, b1_ref, b2_ref, cq_ref, ckv_ref)
    u = _rms(h_ref[...], g_ref[...]).astype(BF16)
    u_ref[...] = u
    layout = _proj_layout()
    heads_per_chunk = PROJ_CHUNK // HEAD_DIM
    staged = 0
    nchunk = IN_WIDTH // PROJ_CHUNK
    cost = {"plain": 0, "rope": 1, "rope_q": 1, "axial_q": 2, "axial_k": 2}
    order = sorted(range(nchunk), key=lambda c: -sum(
        cost[layout[c * heads_per_chunk + hh][0]] for hh in range(heads_per_chunk)))
    for c in order:
        p = jnp.dot(u, w_ref[:, c * PROJ_CHUNK:(c + 1) * PROJ_CHUNK], preferred_element_type=F32)
        for hh in range(heads_per_chunk):
            kind, oi, col = layout[c * heads_per_chunk + hh]
            y = p[:, hh * HEAD_DIM:(hh + 1) * HEAD_DIM]
            if kind in ("rope", "rope_q"):
                y = _rope(y, cos_ref[...], sin_ref[...])
            elif kind in ("axial_q", "axial_k"):
                y = _rms(y, gq_ref[...] if kind == "axial_q" else gk_ref[...])
                y = _axial_rope(y, cosc_ref[...], sinlo_ref[...], sinhi_ref[...])
            if kind == "rope_q":
                y = y * Q_SCALE
            elif kind == "axial_q":
                y = y * (Q_SCALE * LOG2_E)
            dil = B_GROUPS[oi - 1][1] if 1 <= oi <= len(B_GROUPS) else 1
            if dil == 1:
                outs[oi][:, col:col + HEAD_DIM] = y.astype(BF16)
            else:
                slot = staged % PROJ_STAGE_SLOTS
                staged += 1
                stage_ref[slot] = y
                n = y.shape[0] // dil
                for r in range(dil):
                    outs[oi][r, :, col:col + HEAD_DIM] = stage_ref[slot, pl.ds(r, n, stride=dil), :].astype(BF16)


def _proj(h, g, w, gq, gk, tables, bsz, seq, layer):
    t, d = h.shape
    tm = min(TOKEN_BLOCK, seq)
    nseq = seq // tm
    row = lambda i: (i, 0)
    const = lambda i: (0, 0)
    pos = lambda i: (i % nseq, 0)
    out_specs, out_shape = [], []
    for width, dil in ((A_WIDTH, 1),) + tuple((B_GROUP_WIDTH, dl) for _, dl in B_GROUPS) + ((C_Q, 1), (2 * C_KV, 1)):
        if dil == 1:
            out_specs.append(pl.BlockSpec((tm, width), row))
            out_shape.append(jax.ShapeDtypeStruct((t, width), BF16))
        else:
            out_specs.append(pl.BlockSpec((None, dil, tm // dil, width), lambda i: (i // nseq, 0, i % nseq, 0)))
            out_shape.append(jax.ShapeDtypeStruct((bsz, dil, seq // dil, width), BF16))
    out_specs.append(pl.BlockSpec((tm, d), row))
    out_shape.append(jax.ShapeDtypeStruct((t, d), BF16))
    return pl.pallas_call(
        _proj_kernel,
        grid=(t // tm,),
        in_specs=[
            pl.BlockSpec((tm, d), row),
            pl.BlockSpec((1, d), const),
            pl.BlockSpec((None, d, IN_WIDTH), lambda i: (layer, 0, 0), pipeline_mode=pl.Buffered(1)),
            pl.BlockSpec((1, HEAD_DIM), const),
            pl.BlockSpec((1, HEAD_DIM), const),
        ] + [pl.BlockSpec((tm, HEAD_DIM), pos)] * 5,
        out_specs=out_specs,
        out_shape=out_shape,
        scratch_shapes=[pltpu.VMEM((PROJ_STAGE_SLOTS, tm, HEAD_DIM), F32)],
        compiler_params=_params("parallel"),
        name="proj",
    )(h, g, w, gq, gk, *tables)


def _band_kernel(*refs, W, G, HKV, TQ, L, has_sink, want_lse):
    refs = list(refs)
    main_ref, prev_ref, next_ref = refs[:3]
    refs = refs[3:]
    sink_ref = refs.pop(0) if has_sink else None
    o_ref = refs.pop(0)
    lse_ref = refs.pop(0) if want_lse else None

    i = pl.program_id(2)
    QB = min(HEAD_DIM, TQ)
    KW = QB + 2 * W
    nb = TQ // QB

    def head(ref, idx):
        return ref[:, idx * HEAD_DIM:(idx + 1) * HEAD_DIM]

    qs, ks, vs = [], [], []
    for h in range(HKV):
        kh, vh = HKV * G + h, HKV * G + HKV + h
        kcat = jnp.concatenate([head(prev_ref, kh), head(main_ref, kh), head(next_ref, kh)], axis=0)
        vcat = jnp.concatenate([head(prev_ref, vh), head(main_ref, vh), head(next_ref, vh)], axis=0)
        for b in range(nb):
            qg = [main_ref[b * QB:(b + 1) * QB, (h * G + g) * HEAD_DIM:(h * G + g + 1) * HEAD_DIM]
                  for g in range(G)]
            qs.append(jnp.concatenate(qg, axis=0) if G > 1 else qg[0])
            ks.append(kcat[b * QB:b * QB + KW])
            vs.append(vcat[b * QB:b * QB + KW])
    q3, k3, v3 = jnp.stack(qs), jnp.stack(ks), jnp.stack(vs)

    shape = (HKV * nb, G * QB, KW)
    blk = lax.broadcasted_iota(jnp.int32, shape, 0) % nb
    qrow = lax.broadcasted_iota(jnp.int32, shape, 1) % QB
    kcol = lax.broadcasted_iota(jnp.int32, shape, 2)
    kpos = kcol + blk * QB + (i * TQ - W)
    valid = jnp.logical_and(jnp.abs(kcol - W - qrow) <= W, jnp.logical_and(kpos >= 0, kpos < L))

    s = jnp.einsum("bqd,bkd->bqk", q3, k3, preferred_element_type=F32)
    s = jnp.where(valid, s, MASK_VALUE)
    m = jnp.max(s, axis=-1, keepdims=True)
    p = jnp.exp(s - m)
    l = jnp.sum(p, axis=-1, keepdims=True)
    if has_sink:
        sink = jnp.concatenate([
            jnp.concatenate([jnp.full((nb, QB, 1), sink_ref[h * G + g], F32) for g in range(G)], axis=1)
            for h in range(HKV)], axis=0)
        l = l + jnp.exp(sink - m)
    o = jnp.einsum("bqk,bkd->bqd", p.astype(BF16), v3, preferred_element_type=F32) / l
    if want_lse:
        lse = jnp.broadcast_to(m + jnp.log(l), o.shape)
    for h in range(HKV):
        for b in range(nb):
            for g in range(G):
                col = (h * G + g) * HEAD_DIM
                dst = (slice(b * QB, (b + 1) * QB), slice(col, col + HEAD_DIM))
                o_ref[dst] = o[h * nb + b, g * QB:(g + 1) * QB].astype(o_ref.dtype)
                if want_lse:
                    lse_ref[dst] = lse[h * nb + b, g * QB:(g + 1) * QB]


def _band(x, sink, *, W, G, HKV, rows, want_lse):
    bsz, R, L, cw = x.shape
    assert cw == (HKV * G + 2 * HKV) * HEAD_DIM
    ow = HKV * G * HEAD_DIM
    tq = min(rows, L)
    nblk, per = L // tq, tq // W
    has_sink = sink is not None
    in_specs = [
        pl.BlockSpec((None, None, tq, cw), lambda b, r, i: (b, r, i, 0)),
        pl.BlockSpec((None, None, W, cw), lambda b, r, i: (b, r, jnp.maximum(i * per - 1, 0), 0)),
        pl.BlockSpec((None, None, W, cw), lambda b, r, i: (b, r, jnp.minimum((i + 1) * per, L // W - 1), 0)),
    ]
    args = [x, x, x]
    if has_sink:
        in_specs.append(pl.BlockSpec(memory_space=pltpu.SMEM))
        args.append(sink)
    out_spec = pl.BlockSpec((None, None, tq, ow), lambda b, r, i: (b, r, i, 0))
    out_specs, out_shape = [out_spec], [jax.ShapeDtypeStruct((bsz, R, L, ow), BF16)]
    if want_lse:
        out_specs.append(out_spec)
        out_shape.append(jax.ShapeDtypeStruct((bsz, R, L, ow), F32))
    return pl.pallas_call(
        functools.partial(_band_kernel, W=W, G=G, HKV=HKV, TQ=tq, L=L, has_sink=has_sink, want_lse=want_lse),
        grid=(bsz, R, nblk),
        in_specs=in_specs,
        out_specs=out_specs,
        out_shape=out_shape,
        compiler_params=_params("parallel", "parallel", "parallel"),
        name="band_sink" if has_sink else f"band_dil{R}",
    )(*args)


ONES_ROWS = 16


def _flash_kernel(q_ref, k_ref, v_ref, o_ref, qs_ref, vt_ref, s_ref, m_ref, acc_ref, *, TQ, TK, G, S):
    nchunks = S // TK

    def rows(chunk):
        return pl.ds(pl.multiple_of(chunk * TK, TK), TK)

    @pl.when(pl.program_id(2) == 0)
    def _():
        vt_ref[HEAD_DIM:, :] = jnp.ones((ONES_ROWS, S), BF16)

        def fill(c, carry):
            vt_ref[:HEAD_DIM, rows(c)] = v_ref[0, rows(c), :].astype(F32).T.astype(BF16)
            return carry

        lax.fori_loop(0, nchunks, fill, 0)

    for g in range(G):
        qs_ref[g * TQ:(g + 1) * TQ, :] = q_ref[0, :, g * HEAD_DIM:(g + 1) * HEAD_DIM]
    m_ref[...] = jnp.full(m_ref.shape, MASK_VALUE, F32)
    acc_ref[...] = jnp.zeros(acc_ref.shape, F32)

    def scores(chunk, slot):
        s_ref[slot] = lax.dot_general(k_ref[0, rows(chunk), :], qs_ref[...], (((1,), (1,)), ((), ())),
                                      preferred_element_type=F32)

    def accumulate(chunk, slot):
        s = s_ref[slot]
        m_prev = m_ref[...]
        m_next = jnp.maximum(m_prev, jnp.max(s, axis=0, keepdims=True))
        alpha = jnp.exp2(m_prev - m_next)
        p = jnp.exp2(s - m_next).astype(BF16)
        acc_ref[...] = alpha * acc_ref[...] + jnp.dot(vt_ref[:, rows(chunk)], p, preferred_element_type=F32)
        m_ref[...] = m_next

    scores(0, 0)

    def step(j, carry):
        for u in range(FLASH_UNROLL):
            c = FLASH_UNROLL * j + u
            nxt = c + 1 if u < FLASH_UNROLL - 1 else jnp.minimum(c + 1, nchunks - 1)
            scores(nxt, (u + 1) % 2)
            accumulate(c, u % 2)
        return carry

    lax.fori_loop(0, nchunks // FLASH_UNROLL, step, 0)
    out = (acc_ref[:HEAD_DIM, :] / acc_ref[HEAD_DIM:HEAD_DIM + 1, :]).T
    for g in range(G):
        o_ref[0, :, g * HEAD_DIM:(g + 1) * HEAD_DIM] = out[g * TQ:(g + 1) * TQ].astype(BF16)


def _flash(q, kv):
    bsz, S, _ = q.shape
    G = C_Q_HEADS // C_KV_HEADS
    tq = min(FLASH_Q_ROWS, S)
    tk = min(FLASH_K_ROWS, S // FLASH_UNROLL)
    assert S % (tk * FLASH_UNROLL) == 0 and FLASH_UNROLL % 2 == 0
    gw = G * HEAD_DIM
    return pl.pallas_call(
        functools.partial(_flash_kernel, TQ=tq, TK=tk, G=G, S=S),
        grid=(bsz, C_KV_HEADS, S // tq),
        in_specs=[
            pl.BlockSpec((1, tq, gw), lambda b, h, i: (b, i, h)),
            pl.BlockSpec((1, S, HEAD_DIM), lambda b, h, i: (b, 0, h)),
            pl.BlockSpec((1, S, HEAD_DIM), lambda b, h, i: (b, 0, C_KV_HEADS + h)),
        ],
        out_specs=pl.BlockSpec((1, tq, gw), lambda b, h, i: (b, i, h)),
        out_shape=jax.ShapeDtypeStruct((bsz, S, C_Q), BF16),
        scratch_shapes=[
            pltpu.VMEM((G * tq, HEAD_DIM), BF16),
            pltpu.VMEM((HEAD_DIM + ONES_ROWS, S), BF16),
            pltpu.VMEM((2, tk, G * tq), F32),
            pltpu.VMEM((1, G * tq), F32),
            pltpu.VMEM((HEAD_DIM + ONES_ROWS, G * tq), F32),
        ],
        compiler_params=_params("parallel", "parallel", "arbitrary"),
        name="flash",
    )(q, kv, kv)


def _mix_kernel(h_ref, u_ref, ya_ref, ob0_ref, ob1_ref, ob2_ref, ls0_ref, ls1_ref, ls2_ref, yc_ref,
                wg0_ref, wg1_ref, wg2_ref, bg0_ref, bg1_ref, bg2_ref, wba_ref, wbb_ref, wbc_ref, wo_ref,
                o_ref, yb_ref, tok_ref):
    c = pl.program_id(1)

    @pl.when(c == 0)
    def _():
        o_ref[...] = h_ref[...]
        obs = (ob0_ref, ob1_ref, ob2_ref)
        lss = (ls0_ref, ls1_ref, ls2_ref)
        gw = B_HEADS_PER_GROUP * HEAD_DIM
        slot = 0

        def token_major(ref, sl, dil, slot):
            if dil == 1:
                return ref[:, sl].astype(F32)
            n = ref.shape[1]
            for r in range(dil):
                tok_ref[slot, pl.ds(r, n, stride=dil), :] = ref[r, :, sl].astype(F32)
            return tok_ref[slot]

        for j in range(B_HEADS_PER_GROUP):
            sl = slice(j * HEAD_DIM, (j + 1) * HEAD_DIM)
            ls, ob = [], []
            for gi, (_, dil) in enumerate(B_GROUPS):
                ls.append(token_major(lss[gi], sl, dil, slot))
                ob.append(token_major(obs[gi], sl, dil, slot + 1))
                slot += 2 if dil > 1 else 0
            m = jnp.maximum(jnp.maximum(ls[0], ls[1]), ls[2])
            es = [jnp.exp(v - m) for v in ls]
            den = es[0] + es[1] + es[2]
            for gi in range(len(B_GROUPS)):
                wt = es[gi] / den
                yb_ref[:, gi * gw + j * HEAD_DIM:gi * gw + (j + 1) * HEAD_DIM] = (ob[gi] * wt).astype(BF16)

    u = u_ref[...]

    def gate(w_ref, b_ref):
        return jax.nn.sigmoid(jnp.dot(u, w_ref[...], preferred_element_type=F32) + b_ref[...])

    merged = gate(wg0_ref, bg0_ref) * jnp.dot(ya_ref[...], wba_ref[...], preferred_element_type=F32)
    merged += gate(wg1_ref, bg1_ref) * jnp.dot(yb_ref[...], wbb_ref[...], preferred_element_type=F32)
    merged += gate(wg2_ref, bg2_ref) * jnp.dot(yc_ref[...], wbc_ref[...], preferred_element_type=F32)
    o_ref[...] += jnp.dot(merged.astype(BF16), wo_ref[...], preferred_element_type=F32)


def _mix(h, u, ya, obs, lss, yc, w_gate, b_gate, wb_a, wb_b, wb_c, w_o, seq, layer):
    t, d = h.shape
    tm = min(TOKEN_BLOCK, seq)
    nseq = seq // tm
    nc = d // MIX_CHUNK
    row = lambda i, c: (i, 0)
    gw = B_HEADS_PER_GROUP * HEAD_DIM
    group_specs = [
        pl.BlockSpec((tm, gw), row) if dil == 1 else
        pl.BlockSpec((None, dil, tm // dil, gw), lambda i, c: (i // nseq, 0, i % nseq, 0))
        for _, dil in B_GROUPS]
    n_stage = 2 * B_HEADS_PER_GROUP * sum(dil > 1 for _, dil in B_GROUPS)
    gate_spec = lambda k: pl.BlockSpec((None, d, MIX_CHUNK), lambda i, c: (layer, 0, k * nc + c))
    bias_spec = lambda k: pl.BlockSpec((1, MIX_CHUNK), lambda i, c: (0, k * nc + c))
    col_spec = lambda rows: pl.BlockSpec((None, rows, MIX_CHUNK), lambda i, c: (layer, 0, c))
    return pl.pallas_call(
        _mix_kernel,
        grid=(t // tm, nc),
        in_specs=[
            pl.BlockSpec((tm, d), row),
            pl.BlockSpec((tm, d), row),
            pl.BlockSpec((tm, A_Q), row),
            *group_specs, *group_specs,
            pl.BlockSpec((tm, C_Q), row),
            gate_spec(0), gate_spec(1), gate_spec(2),
            bias_spec(0), bias_spec(1), bias_spec(2),
            col_spec(A_Q), col_spec(B_W), col_spec(C_Q),
            pl.BlockSpec((None, MIX_CHUNK, d), lambda i, c: (layer, c, 0)),
        ],
        out_specs=pl.BlockSpec((tm, d), row),
        out_shape=jax.ShapeDtypeStruct((t, d), F32),
        scratch_shapes=[pltpu.VMEM((tm, B_W), BF16), pltpu.VMEM((n_stage, tm, HEAD_DIM), F32)],
        compiler_params=_params("parallel", "arbitrary"),
        name="mix",
    )(h, u, ya, *obs, *lss, yc, w_gate, w_gate, w_gate, b_gate, b_gate, b_gate, wb_a, wb_b, wb_c, w_o)


def _rope_tables(seq):
    def angles(pos, dim):
        inv = ROPE_THETA ** (-jnp.arange(0, dim, 2, dtype=F32) / dim)
        return pos.astype(F32)[:, None] * inv[None, :]

    pos = jnp.arange(seq)
    ang = angles(pos, HEAD_DIM)
    cos, sin = jnp.cos(ang), jnp.sin(ang)
    ar = angles(pos // GRID_W, HEAD_DIM // 2)
    ac = angles(pos % GRID_W, HEAD_DIM // 2)
    cr, sr, cc, sc = jnp.cos(ar), jnp.sin(ar), jnp.cos(ac), jnp.sin(ac)
    z = jnp.zeros_like(sr)
    return (
        jnp.concatenate([cos, cos], -1),
        jnp.concatenate([-sin, sin], -1),
        jnp.concatenate([cr, cr, cc, cc], -1),
        jnp.concatenate([-sr, z, -sc, z], -1),
        jnp.concatenate([z, sr, z, sc], -1),
    )


def _prepare_weights(g_ffn1, w_ffn1_in, w_ffn1_out, g_mix, w_in, a_sink, c_q_norm, c_k_norm,
                     w_branch, w_gate, b_gate, w_o, g_ffn2, w_ffn2_in, w_ffn2_out):
    bf = lambda w: w.astype(BF16)
    depth = g_ffn1.shape[0]
    rows = lambda v: [v[l].reshape(1, -1) for l in range(depth)]
    return dict(
        g_ffn1=rows(g_ffn1), w_ffn1_in=bf(w_ffn1_in), w_ffn1_out=bf(w_ffn1_out),
        g_mix=rows(g_mix), w_in=bf(w_in), a_sink=[a_sink[l].reshape(-1) for l in range(depth)],
        c_q_norm=rows(c_q_norm), c_k_norm=rows(c_k_norm),
        wb_a=bf(w_branch[:, :A_Q]), wb_b=bf(w_branch[:, A_Q:A_Q + B_W]), wb_c=bf(w_branch[:, A_Q + B_W:]),
        w_gate=bf(w_gate), b_gate=rows(b_gate), w_o=bf(w_o),
        g_ffn2=rows(g_ffn2), w_ffn2_in=bf(w_ffn2_in), w_ffn2_out=bf(w_ffn2_out),
    )


def _trunk(x, w, g_final):
    bsz, seq, d = x.shape
    t = bsz * seq
    depth = len(w["g_ffn1"])
    tables = _rope_tables(seq)
    gw = B_HEADS_PER_GROUP * HEAD_DIM
    x = x.reshape(t, d)
    for l in range(depth):
        h = _ffn(x, w["g_ffn1"][l], w["w_ffn1_in"], w["w_ffn1_out"], g_final, layer=l, final_norm=False)
        a, b0, b1, b2, cq, ckv, u = _proj(h, w["g_mix"][l], w["w_in"], w["c_q_norm"][l], w["c_k_norm"][l],
                                          tables, bsz, seq, l)
        (ya,) = _band(a.reshape(bsz, 1, seq, A_WIDTH), w["a_sink"][l], W=A_HALF_WINDOW,
                      G=A_Q_HEADS // A_KV_HEADS, HKV=A_KV_HEADS, rows=BAND_ROWS, want_lse=False)
        obs, lss = [], []
        for bg, (window, dil) in zip((b0, b1, b2), B_GROUPS):
            o, ls = _band(bg.reshape(bsz, dil, seq // dil, B_GROUP_WIDTH), None, W=window // (2 * dil),
                          G=1, HKV=B_HEADS_PER_GROUP, rows=BAND_ROWS_DILATED, want_lse=True)
            obs.append(o.reshape(t, gw) if dil == 1 else o)
            lss.append(ls.reshape(t, gw) if dil == 1 else ls)
        yc = _flash(cq.reshape(bsz, seq, C_Q), ckv.reshape(bsz, seq, 2 * C_KV))
        h = _mix(h, u, ya.reshape(t, A_Q), obs, lss, yc.reshape(t, C_Q),
                 w["w_gate"], w["b_gate"][l], w["wb_a"], w["wb_b"], w["wb_c"], w["w_o"], seq, l)
        x = _ffn(h, w["g_ffn2"][l], w["w_ffn2_in"], w["w_ffn2_out"], g_final, layer=l,
                 final_norm=(l == depth - 1))
    return x.reshape(bsz, seq, d)


def kernel(x_prompt, x_sample, g_ffn1, w_ffn1_in, w_ffn1_out, g_mix, w_in, a_sink, c_q_norm, c_k_norm,
           w_branch, w_gate, b_gate, w_o, g_ffn2, w_ffn2_in, w_ffn2_out, g_final):
    w = _prepare_weights(g_ffn1, w_ffn1_in, w_ffn1_out, g_mix, w_in, a_sink, c_q_norm, c_k_norm,
                         w_branch, w_gate, b_gate, w_o, g_ffn2, w_ffn2_in, w_ffn2_out)
    gf = g_final.reshape(1, -1)
    return _trunk(x_prompt, w, gf), _trunk(x_sample, w, gf)
```

```python
import functools

import jax
import jax.numpy as jnp
from jax import lax
from jax.experimental import pallas as pl
from jax.experimental.pallas import tpu as pltpu

F32 = jnp.float32
BF16 = jnp.bfloat16

D_MODEL = 2048
D_FF = 5632
HEAD_DIM = 128
A_Q_HEADS, A_KV_HEADS, A_HALF_WINDOW = 4, 2, 128
B_GROUPS = ((128, 1), (512, 4), (2048, 16))
B_HEADS_PER_GROUP = 2
C_Q_HEADS, C_KV_HEADS = 6, 2
GRID_W = 64
ROPE_THETA = 10000.0
N_BRANCH = 3
NORM_EPS = 1e-6
MASK_VALUE = -1e30
Q_SCALE = HEAD_DIM ** -0.5
LOG2_E = 1.4426950408889634

A_Q = A_Q_HEADS * HEAD_DIM
A_KV = A_KV_HEADS * HEAD_DIM
B_W = B_HEADS_PER_GROUP * len(B_GROUPS) * HEAD_DIM
C_Q = C_Q_HEADS * HEAD_DIM
C_KV = C_KV_HEADS * HEAD_DIM
A_WIDTH = A_Q + 2 * A_KV
B_GROUP_WIDTH = 3 * B_HEADS_PER_GROUP * HEAD_DIM
IN_WIDTH = A_WIDTH + 3 * B_W + C_Q + 2 * C_KV

VMEM_LIMIT_BYTES = 58 * 1024 * 1024

TOKEN_BLOCK = 512
FFN_TOKEN_BLOCK = 1024
FF_CHUNK = 512
PROJ_CHUNK = 512
PROJ_STAGE_SLOTS = 4
MIX_CHUNK = 512
MIX_TOKEN_BLOCK = 1024
OUT_TOKEN_BLOCK = 1024
BAND_ROWS = 512
BAND_ROWS_DILATED = 2048
FLASH_Q_ROWS = 256
FLASH_K_ROWS = 512
FLASH_UNROLL = 8


def _params(*sem):
    return pltpu.CompilerParams(dimension_semantics=sem, vmem_limit_bytes=VMEM_LIMIT_BYTES)


def _rms(x, g):
    return x * lax.rsqrt(jnp.mean(x * x, axis=-1, keepdims=True) + NORM_EPS) * g


def _ffn_kernel(x_ref, g_ref, wg_ref, wu_ref, wo_ref, gf_ref, o_ref, n_ref, *, final_norm):
    j = pl.program_id(1)

    @pl.when(j == 0)
    def _():
        x = x_ref[...]
        n_ref[...] = _rms(x, g_ref[...]).astype(BF16)
        o_ref[...] = x

    n = n_ref[...]
    gate = jnp.dot(n, wg_ref[...], preferred_element_type=F32)
    up = jnp.dot(n, wu_ref[...], preferred_element_type=F32)
    a = (gate * jax.nn.sigmoid(gate) * (0.5 * up)).astype(BF16)
    o_ref[...] += jnp.dot(a, wo_ref[...], preferred_element_type=F32)

    if final_norm:
        @pl.when(j == pl.num_programs(1) - 1)
        def _():
            o_ref[...] = _rms(o_ref[...], gf_ref[...])


def _ffn(x, g, w_in, w_out, g_final, *, layer, final_norm):
    t, d = x.shape
    tm = min(FFN_TOKEN_BLOCK, t)
    nf = D_FF // FF_CHUNK
    return pl.pallas_call(
        functools.partial(_ffn_kernel, final_norm=final_norm),
        grid=(t // tm, nf),
        in_specs=[
            pl.BlockSpec((tm, d), lambda i, j: (i, 0)),
            pl.BlockSpec((1, d), lambda i, j: (0, 0)),
            pl.BlockSpec((None, d, FF_CHUNK), lambda i, j: (layer, 0, j)),
            pl.BlockSpec((None, d, FF_CHUNK), lambda i, j: (layer, 0, j + nf)),
            pl.BlockSpec((None, FF_CHUNK, d), lambda i, j: (layer, j, 0)),
            pl.BlockSpec((1, d), lambda i, j: (0, 0)),
        ],
        out_specs=pl.BlockSpec((tm, d), lambda i, j: (i, 0)),
        out_shape=jax.ShapeDtypeStruct((t, d), F32),
        scratch_shapes=[pltpu.VMEM((tm, d), BF16)],
        compiler_params=_params("parallel", "arbitrary"),
        name="ffn_final" if final_norm else "ffn",
    )(x, g, w_in, w_in, w_out, g_final)


def _rope(y, cos, sin):
    return y * cos + pltpu.roll(y, HEAD_DIM // 2, 1) * sin


def _axial_rope(y, cos, sin_lo, sin_hi):
    q = HEAD_DIM // 4
    return y * cos + pltpu.roll(y, HEAD_DIM - q, 1) * sin_lo + pltpu.roll(y, q, 1) * sin_hi


def _proj_layout():
    heads = []
    hpg = B_HEADS_PER_GROUP
    for h in range(A_Q_HEADS):
        heads.append(("rope_q", 0, h * HEAD_DIM))
    for h in range(A_KV_HEADS):
        heads.append(("rope", 0, A_Q + h * HEAD_DIM))
    for h in range(A_KV_HEADS):
        heads.append(("plain", 0, A_Q + A_KV + h * HEAD_DIM))
    for part, kind in enumerate(("rope_q", "rope", "plain")):
        for h in range(hpg * len(B_GROUPS)):
            heads.append((kind, 1 + h // hpg, (part * hpg + h % hpg) * HEAD_DIM))
    for h in range(C_Q_HEADS):
        heads.append(("axial_q", 4, h * HEAD_DIM))
    for h in range(C_KV_HEADS):
        heads.append(("axial_k", 5, h * HEAD_DIM))
    for h in range(C_KV_HEADS):
        heads.append(("plain", 5, C_KV + h * HEAD_DIM))
    assert len(heads) * HEAD_DIM == IN_WIDTH
    return heads


def _proj_kernel(h_ref, g_ref, w_ref, gq_ref, gk_ref, cos_ref, sin_ref, cosc_ref, sinlo_ref, sinhi_ref,
                 a_ref, b0_ref, b1_ref, b2_ref, cq_ref, ckv_ref, u_ref, stage_ref):
    outs = (a_ref, b0_ref, b1_ref, b2_ref, cq_ref, ckv_ref)
    u = _rms(h_ref[...], g_ref[...]).astype(BF16)
    u_ref[...] = u
    layout = _proj_layout()
    heads_per_chunk = PROJ_CHUNK // HEAD_DIM
    staged = 0
    nchunk = IN_WIDTH // PROJ_CHUNK
    cost = {"plain": 0, "rope": 1, "rope_q": 1, "axial_q": 2, "axial_k": 2}
    order = sorted(range(nchunk), key=lambda c: -sum(
        cost[layout[c * heads_per_chunk + hh][0]] for hh in range(heads_per_chunk)))
    for c in order:
        p = jnp.dot(u, w_ref[:, c * PROJ_CHUNK:(c + 1) * PROJ_CHUNK], preferred_element_type=F32)
        for hh in range(heads_per_chunk):
            kind, oi, col = layout[c * heads_per_chunk + hh]
            y = p[:, hh * HEAD_DIM:(hh + 1) * HEAD_DIM]
            if kind in ("rope", "rope_q"):
                y = _rope(y, cos_ref[...], sin_ref[...])
            elif kind in ("axial_q", "axial_k"):
                y = _rms(y, gq_ref[...] if kind == "axial_q" else gk_ref[...])
                y = _axial_rope(y, cosc_ref[...], sinlo_ref[...], sinhi_ref[...])
            if kind == "rope_q":
                y = y * Q_SCALE
            elif kind == "axial_q":
                y = y * (Q_SCALE * LOG2_E)
            dil = B_GROUPS[oi - 1][1] if 1 <= oi <= len(B_GROUPS) else 1
            if dil == 1:
                outs[oi][:, col:col + HEAD_DIM] = y.astype(BF16)
            else:
                slot = staged % PROJ_STAGE_SLOTS
                staged += 1
                stage_ref[slot] = y
                n = y.shape[0] // dil
                for r in range(dil):
                    outs[oi][r, :, col:col + HEAD_DIM] = stage_ref[slot, pl.ds(r, n, stride=dil), :].astype(BF16)


def _proj(h, g, w, gq, gk, tables, bsz, seq, layer):
    t, d = h.shape
    tm = min(TOKEN_BLOCK, seq)
    nseq = seq // tm
    row = lambda i: (i, 0)
    const = lambda i: (0, 0)
    pos = lambda i: (i % nseq, 0)
    out_specs, out_shape = [], []
    for width, dil in ((A_WIDTH, 1),) + tuple((B_GROUP_WIDTH, dl) for _, dl in B_GROUPS) + ((C_Q, 1), (2 * C_KV, 1)):
        if dil == 1:
            out_specs.append(pl.BlockSpec((tm, width), row))
            out_shape.append(jax.ShapeDtypeStruct((t, width), BF16))
        else:
            out_specs.append(pl.BlockSpec((None, dil, tm // dil, width), lambda i: (i // nseq, 0, i % nseq, 0)))
            out_shape.append(jax.ShapeDtypeStruct((bsz, dil, seq // dil, width), BF16))
    out_specs.append(pl.BlockSpec((tm, d), row))
    out_shape.append(jax.ShapeDtypeStruct((t, d), BF16))
    return pl.pallas_call(
        _proj_kernel,
        grid=(t // tm,),
        in_specs=[
            pl.BlockSpec((tm, d), row),
            pl.BlockSpec((1, d), const),
            pl.BlockSpec((None, d, IN_WIDTH), lambda i: (layer, 0, 0), pipeline_mode=pl.Buffered(1)),
            pl.BlockSpec((1, HEAD_DIM), const),
            pl.BlockSpec((1, HEAD_DIM), const),
        ] + [pl.BlockSpec((tm, HEAD_DIM), pos)] * 5,
        out_specs=out_specs,
        out_shape=out_shape,
        scratch_shapes=[pltpu.VMEM((PROJ_STAGE_SLOTS, tm, HEAD_DIM), F32)],
        compiler_params=_params("parallel"),
        name="proj",
    )(h, g, w, gq, gk, *tables)


def _band_kernel(*refs, W, G, HKV, TQ, L, has_sink, want_lse):
    refs = list(refs)
    main_ref, prev_ref, next_ref = refs[:3]
    refs = refs[3:]
    sink_ref = refs.pop(0) if has_sink else None
    o_ref = refs.pop(0)
    lse_ref = refs.pop(0) if want_lse else None

    i = pl.program_id(2)
    QB = min(HEAD_DIM, TQ)
    KW = QB + 2 * W
    nb = TQ // QB

    def head(ref, idx):
        return ref[:, idx * HEAD_DIM:(idx + 1) * HEAD_DIM]

    qs, ks, vs = [], [], []
    for h in range(HKV):
        kh, vh = HKV * G + h, HKV * G + HKV + h
        kcat = jnp.concatenate([head(prev_ref, kh), head(main_ref, kh), head(next_ref, kh)], axis=0)
        vcat = jnp.concatenate([head(prev_ref, vh), head(main_ref, vh), head(next_ref, vh)], axis=0)
        for b in range(nb):
            qg = [main_ref[b * QB:(b + 1) * QB, (h * G + g) * HEAD_DIM:(h * G + g + 1) * HEAD_DIM]
                  for g in range(G)]
            qs.append(jnp.concatenate(qg, axis=0) if G > 1 else qg[0])
            ks.append(kcat[b * QB:b * QB + KW])
            vs.append(vcat[b * QB:b * QB + KW])
    q3, k3, v3 = jnp.stack(qs), jnp.stack(ks), jnp.stack(vs)

    shape = (HKV * nb, G * QB, KW)
    blk = lax.broadcasted_iota(jnp.int32, shape, 0) % nb
    qrow = lax.broadcasted_iota(jnp.int32, shape, 1) % QB
    kcol = lax.broadcasted_iota(jnp.int32, shape, 2)
    kpos = kcol + blk * QB + (i * TQ - W)
    valid = jnp.logical_and(jnp.abs(kcol - W - qrow) <= W, jnp.logical_and(kpos >= 0, kpos < L))

    s = jnp.einsum("bqd,bkd->bqk", q3, k3, preferred_element_type=F32)
    s = jnp.where(valid, s, MASK_VALUE)
    m = jnp.max(s, axis=-1, keepdims=True)
    p = jnp.exp(s - m)
    l = jnp.sum(p, axis=-1, keepdims=True)
    if has_sink:
        sink = jnp.concatenate([
            jnp.concatenate([jnp.full((nb, QB, 1), sink_ref[h * G + g], F32) for g in range(G)], axis=1)
            for h in range(HKV)], axis=0)
        l = l + jnp.exp(sink - m)
    o = jnp.einsum("bqk,bkd->bqd", p.astype(BF16), v3, preferred_element_type=F32) / l
    if want_lse:
        lse = jnp.broadcast_to(m + jnp.log(l), o.shape)
    for h in range(HKV):
        for b in range(nb):
            for g in range(G):
                col = (h * G + g) * HEAD_DIM
                dst = (slice(b * QB, (b + 1) * QB), slice(col, col + HEAD_DIM))
                o_ref[dst] = o[h * nb + b, g * QB:(g + 1) * QB].astype(o_ref.dtype)
                if want_lse:
                    lse_ref[dst] = lse[h * nb + b, g * QB:(g + 1) * QB]


def _band(x, sink, *, W, G, HKV, rows, want_lse):
    bsz, R, L, cw = x.shape
    assert cw == (HKV * G + 2 * HKV) * HEAD_DIM
    ow = HKV * G * HEAD_DIM
    tq = min(rows, L)
    nblk, per = L // tq, tq // W
    has_sink = sink is not None
    in_specs = [
        pl.BlockSpec((None, None, tq, cw), lambda b, r, i: (b, r, i, 0)),
        pl.BlockSpec((None, None, W, cw), lambda b, r, i: (b, r, jnp.maximum(i * per - 1, 0), 0)),
        pl.BlockSpec((None, None, W, cw), lambda b, r, i: (b, r, jnp.minimum((i + 1) * per, L // W - 1), 0)),
    ]
    args = [x, x, x]
    if has_sink:
        in_specs.append(pl.BlockSpec(memory_space=pltpu.SMEM))
        args.append(sink)
    out_spec = pl.BlockSpec((None, None, tq, ow), lambda b, r, i: (b, r, i, 0))
    out_specs, out_shape = [out_spec], [jax.ShapeDtypeStruct((bsz, R, L, ow), BF16)]
    if want_lse:
        out_specs.append(out_spec)
        out_shape.append(jax.ShapeDtypeStruct((bsz, R, L, ow), F32))
    return pl.pallas_call(
        functools.partial(_band_kernel, W=W, G=G, HKV=HKV, TQ=tq, L=L, has_sink=has_sink, want_lse=want_lse),
        grid=(bsz, R, nblk),
        in_specs=in_specs,
        out_specs=out_specs,
        out_shape=out_shape,
        compiler_params=_params("parallel", "parallel", "parallel"),
        name="band_sink" if has_sink else f"band_dil{R}",
    )(*args)


ONES_ROWS = 16


def _flash_kernel(q_ref, k_ref, v_ref, o_ref, qs_ref, vt_ref, s_ref, m_ref, acc_ref, *, TQ, TK, G, S):
    nchunks = S // TK

    def rows(chunk):
        return pl.ds(pl.multiple_of(chunk * TK, TK), TK)

    @pl.when(pl.program_id(2) == 0)
    def _():
        vt_ref[HEAD_DIM:, :] = jnp.ones((ONES_ROWS, S), BF16)

        def fill(c, carry):
            vt_ref[:HEAD_DIM, rows(c)] = v_ref[0, rows(c), :].astype(F32).T.astype(BF16)
            return carry

        lax.fori_loop(0, nchunks, fill, 0)

    for g in range(G):
        qs_ref[g * TQ:(g + 1) * TQ, :] = q_ref[0, :, g * HEAD_DIM:(g + 1) * HEAD_DIM]
    m_ref[...] = jnp.full(m_ref.shape, MASK_VALUE, F32)
    acc_ref[...] = jnp.zeros(acc_ref.shape, F32)

    def scores(chunk, slot):
        s_ref[slot] = lax.dot_general(k_ref[0, rows(chunk), :], qs_ref[...], (((1,), (1,)), ((), ())),
                                      preferred_element_type=F32)

    def accumulate(chunk, slot):
        s = s_ref[slot]
        m_prev = m_ref[...]
        m_next = jnp.maximum(m_prev, jnp.max(s, axis=0, keepdims=True))
        alpha = jnp.exp2(m_prev - m_next)
        p = jnp.exp2(s - m_next).astype(BF16)
        acc_ref[...] = alpha * acc_ref[...] + jnp.dot(vt_ref[:, rows(chunk)], p, preferred_element_type=F32)
        m_ref[...] = m_next

    scores(0, 0)

    def step(j, carry):
        for u in range(FLASH_UNROLL):
            c = FLASH_UNROLL * j + u
            nxt = c + 1 if u < FLASH_UNROLL - 1 else jnp.minimum(c + 1, nchunks - 1)
            scores(nxt, (u + 1) % 2)
            accumulate(c, u % 2)
        return carry

    lax.fori_loop(0, nchunks // FLASH_UNROLL, step, 0)
    out = (acc_ref[:HEAD_DIM, :] / acc_ref[HEAD_DIM:HEAD_DIM + 1, :]).T
    for g in range(G):
        o_ref[0, :, g * HEAD_DIM:(g + 1) * HEAD_DIM] = out[g * TQ:(g + 1) * TQ].astype(BF16)


def _flash(q, kv):
    bsz, S, _ = q.shape
    G = C_Q_HEADS // C_KV_HEADS
    tq = min(FLASH_Q_ROWS, S)
    tk = min(FLASH_K_ROWS, S // FLASH_UNROLL)
    assert S % (tk * FLASH_UNROLL) == 0 and FLASH_UNROLL % 2 == 0
    gw = G * HEAD_DIM
    return pl.pallas_call(
        functools.partial(_flash_kernel, TQ=tq, TK=tk, G=G, S=S),
        grid=(bsz, C_KV_HEADS, S // tq),
        in_specs=[
            pl.BlockSpec((1, tq, gw), lambda b, h, i: (b, i, h)),
            pl.BlockSpec((1, S, HEAD_DIM), lambda b, h, i: (b, 0, h)),
            pl.BlockSpec((1, S, HEAD_DIM), lambda b, h, i: (b, 0, C_KV_HEADS + h)),
        ],
        out_specs=pl.BlockSpec((1, tq, gw), lambda b, h, i: (b, i, h)),
        out_shape=jax.ShapeDtypeStruct((bsz, S, C_Q), BF16),
        scratch_shapes=[
            pltpu.VMEM((G * tq, HEAD_DIM), BF16),
            pltpu.VMEM((HEAD_DIM + ONES_ROWS, S), BF16),
            pltpu.VMEM((2, tk, G * tq), F32),
            pltpu.VMEM((1, G * tq), F32),
            pltpu.VMEM((HEAD_DIM + ONES_ROWS, G * tq), F32),
        ],
        compiler_params=_params("parallel", "parallel", "arbitrary"),
        name="flash",
    )(q, kv, kv)


def _mix_kernel(u_ref, ya_ref, ob0_ref, ob1_ref, ob2_ref, ls0_ref, ls1_ref, ls2_ref, yc_ref,
                wg0_ref, wg1_ref, wg2_ref, bg0_ref, bg1_ref, bg2_ref, wba_ref, wbb_ref, wbc_ref,
                o_ref, yb_ref, tok_ref):
    c = pl.program_id(1)

    @pl.when(c == 0)
    def _():
        obs = (ob0_ref, ob1_ref, ob2_ref)
        lss = (ls0_ref, ls1_ref, ls2_ref)
        gw = B_HEADS_PER_GROUP * HEAD_DIM
        slot = 0

        def token_major(ref, sl, dil, slot):
            if dil == 1:
                return ref[:, sl].astype(F32)
            n = ref.shape[1]
            for r in range(dil):
                tok_ref[slot, pl.ds(r, n, stride=dil), :] = ref[r, :, sl].astype(F32)
            return tok_ref[slot]

        for j in range(B_HEADS_PER_GROUP):
            sl = slice(j * HEAD_DIM, (j + 1) * HEAD_DIM)
            ls, ob = [], []
            for gi, (_, dil) in enumerate(B_GROUPS):
                ls.append(token_major(lss[gi], sl, dil, slot))
                ob.append(token_major(obs[gi], sl, dil, slot + 1))
                slot += 2 if dil > 1 else 0
            m = jnp.maximum(jnp.maximum(ls[0], ls[1]), ls[2])
            es = [jnp.exp(v - m) for v in ls]
            den = es[0] + es[1] + es[2]
            for gi in range(len(B_GROUPS)):
                wt = es[gi] / den
                yb_ref[:, gi * gw + j * HEAD_DIM:gi * gw + (j + 1) * HEAD_DIM] = (ob[gi] * wt).astype(BF16)

    u = u_ref[...]

    def gate(w_ref, b_ref):
        return jax.nn.sigmoid(jnp.dot(u, w_ref[...], preferred_element_type=F32) + b_ref[...])

    merged = gate(wg0_ref, bg0_ref) * jnp.dot(ya_ref[...], wba_ref[...], preferred_element_type=F32)
    merged += gate(wg1_ref, bg1_ref) * jnp.dot(yb_ref[...], wbb_ref[...], preferred_element_type=F32)
    merged += gate(wg2_ref, bg2_ref) * jnp.dot(yc_ref[...], wbc_ref[...], preferred_element_type=F32)
    o_ref[...] = merged.astype(BF16)


def _mix(u, ya, obs, lss, yc, w_gate, b_gate, wb_a, wb_b, wb_c, seq, layer):
    t, d = u.shape
    tm = min(MIX_TOKEN_BLOCK, seq)
    nseq = seq // tm
    nc = d // MIX_CHUNK
    row = lambda i, c: (i, 0)
    gw = B_HEADS_PER_GROUP * HEAD_DIM
    group_specs = [
        pl.BlockSpec((tm, gw), row) if dil == 1 else
        pl.BlockSpec((None, dil, tm // dil, gw), lambda i, c: (i // nseq, 0, i % nseq, 0))
        for _, dil in B_GROUPS]
    n_stage = 2 * B_HEADS_PER_GROUP * sum(dil > 1 for _, dil in B_GROUPS)
    gate_spec = lambda k: pl.BlockSpec((None, d, MIX_CHUNK), lambda i, c: (layer, 0, k * nc + c))
    bias_spec = lambda k: pl.BlockSpec((1, MIX_CHUNK), lambda i, c: (0, k * nc + c))
    col_spec = lambda rows: pl.BlockSpec((None, rows, MIX_CHUNK), lambda i, c: (layer, 0, c))
    return pl.pallas_call(
        _mix_kernel,
        grid=(t // tm, nc),
        in_specs=[
            pl.BlockSpec((tm, d), row),
            pl.BlockSpec((tm, A_Q), row),
            *group_specs, *group_specs,
            pl.BlockSpec((tm, C_Q), row),
            gate_spec(0), gate_spec(1), gate_spec(2),
            bias_spec(0), bias_spec(1), bias_spec(2),
            col_spec(A_Q), col_spec(B_W), col_spec(C_Q),
        ],
        out_specs=pl.BlockSpec((tm, MIX_CHUNK), lambda i, c: (i, c)),
        out_shape=jax.ShapeDtypeStruct((t, d), BF16),
        scratch_shapes=[pltpu.VMEM((tm, B_W), BF16), pltpu.VMEM((n_stage, tm, HEAD_DIM), F32)],
        compiler_params=_params("parallel", "arbitrary"),
        name="mix",
    )(u, ya, *obs, *lss, yc, w_gate, w_gate, w_gate, b_gate, b_gate, b_gate, wb_a, wb_b, wb_c)


def _out_proj_kernel(h_ref, m_ref, wo_ref, o_ref):
    o_ref[...] = h_ref[...] + jnp.dot(m_ref[...], wo_ref[...], preferred_element_type=F32)


def _out_proj(h, merged, w_o, layer):
    t, d = h.shape
    tm = min(OUT_TOKEN_BLOCK, t)
    row = lambda i: (i, 0)
    return pl.pallas_call(
        _out_proj_kernel,
        grid=(t // tm,),
        in_specs=[
            pl.BlockSpec((tm, d), row),
            pl.BlockSpec((tm, d), row),
            pl.BlockSpec((None, d, d), lambda i: (layer, 0, 0), pipeline_mode=pl.Buffered(1)),
        ],
        out_specs=pl.BlockSpec((tm, d), row),
        out_shape=jax.ShapeDtypeStruct((t, d), F32),
        compiler_params=_params("parallel"),
        name="out_proj",
    )(h, merged, w_o)


def _rope_tables(seq):
    def angles(pos, dim):
        inv = ROPE_THETA ** (-jnp.arange(0, dim, 2, dtype=F32) / dim)
        return pos.astype(F32)[:, None] * inv[None, :]

    pos = jnp.arange(seq)
    ang = angles(pos, HEAD_DIM)
    cos, sin = jnp.cos(ang), jnp.sin(ang)
    ar = angles(pos // GRID_W, HEAD_DIM // 2)
    ac = angles(pos % GRID_W, HEAD_DIM // 2)
    cr, sr, cc, sc = jnp.cos(ar), jnp.sin(ar), jnp.cos(ac), jnp.sin(ac)
    z = jnp.zeros_like(sr)
    return (
        jnp.concatenate([cos, cos], -1),
        jnp.concatenate([-sin, sin], -1),
        jnp.concatenate([cr, cr, cc, cc], -1),
        jnp.concatenate([-sr, z, -sc, z], -1),
        jnp.concatenate([z, sr, z, sc], -1),
    )


def _prepare_weights(g_ffn1, w_ffn1_in, w_ffn1_out, g_mix, w_in, a_sink, c_q_norm, c_k_norm,
                     w_branch, w_gate, b_gate, w_o, g_ffn2, w_ffn2_in, w_ffn2_out):
    bf = lambda w: w.astype(BF16)
    depth = g_ffn1.shape[0]
    rows = lambda v: [v[l].reshape(1, -1) for l in range(depth)]
    return dict(
        g_ffn1=rows(g_ffn1), w_ffn1_in=bf(w_ffn1_in), w_ffn1_out=bf(w_ffn1_out),
        g_mix=rows(g_mix), w_in=bf(w_in), a_sink=[a_sink[l].reshape(-1) for l in range(depth)],
        c_q_norm=rows(c_q_norm), c_k_norm=rows(c_k_norm),
        wb_a=bf(w_branch[:, :A_Q]), wb_b=bf(w_branch[:, A_Q:A_Q + B_W]), wb_c=bf(w_branch[:, A_Q + B_W:]),
        w_gate=bf(w_gate), b_gate=rows(b_gate), w_o=bf(w_o),
        g_ffn2=rows(g_ffn2), w_ffn2_in=bf(w_ffn2_in), w_ffn2_out=bf(w_ffn2_out),
    )


def _trunk(x, w, g_final):
    bsz, seq, d = x.shape
    t = bsz * seq
    depth = len(w["g_ffn1"])
    tables = _rope_tables(seq)
    gw = B_HEADS_PER_GROUP * HEAD_DIM
    x = x.reshape(t, d)
    for l in range(depth):
        h = _ffn(x, w["g_ffn1"][l], w["w_ffn1_in"], w["w_ffn1_out"], g_final, layer=l, final_norm=False)
        a, b0, b1, b2, cq, ckv, u = _proj(h, w["g_mix"][l], w["w_in"], w["c_q_norm"][l], w["c_k_norm"][l],
                                          tables, bsz, seq, l)
        (ya,) = _band(a.reshape(bsz, 1, seq, A_WIDTH), w["a_sink"][l], W=A_HALF_WINDOW,
                      G=A_Q_HEADS // A_KV_HEADS, HKV=A_KV_HEADS, rows=BAND_ROWS, want_lse=False)
        obs, lss = [], []
        for bg, (window, dil) in zip((b0, b1, b2), B_GROUPS):
            o, ls = _band(bg.reshape(bsz, dil, seq // dil, B_GROUP_WIDTH), None, W=window // (2 * dil),
                          G=1, HKV=B_HEADS_PER_GROUP, rows=BAND_ROWS_DILATED, want_lse=True)
            obs.append(o.reshape(t, gw) if dil == 1 else o)
            lss.append(ls.reshape(t, gw) if dil == 1 else ls)
        yc = _flash(cq.reshape(bsz, seq, C_Q), ckv.reshape(bsz, seq, 2 * C_KV))
        merged = _mix(u, ya.reshape(t, A_Q), obs, lss, yc.reshape(t, C_Q),
                      w["w_gate"], w["b_gate"][l], w["wb_a"], w["wb_b"], w["wb_c"], seq, l)
        h = _out_proj(h, merged, w["w_o"], l)
        x = _ffn(h, w["g_ffn2"][l], w["w_ffn2_in"], w["w_ffn2_out"], g_final, layer=l,
                 final_norm=(l == depth - 1))
    return x.reshape(bsz, seq, d)


def kernel(x_prompt, x_sample, g_ffn1, w_ffn1_in, w_ffn1_out, g_mix, w_in, a_sink, c_q_norm, c_k_norm,
           w_branch, w_gate, b_gate, w_o, g_ffn2, w_ffn2_in, w_ffn2_out, g_final):
    w = _prepare_weights(g_ffn1, w_ffn1_in, w_ffn1_out, g_mix, w_in, a_sink, c_q_norm, c_k_norm,
                         w_branch, w_gate, b_gate, w_o, g_ffn2, w_ffn2_in, w_ffn2_out)
    gf = g_final.reshape(1, -1)
    return _trunk(x_prompt, w, gf), _trunk(x_sample, w, gf)
```

```python
import functools

import jax
import jax.numpy as jnp
from jax import lax
from jax.experimental import pallas as pl
from jax.experimental.pallas import tpu as pltpu

F32 = jnp.float32
BF16 = jnp.bfloat16

D_MODEL = 2048
D_FF = 5632
HEAD_DIM = 128
A_Q_HEADS, A_KV_HEADS, A_HALF_WINDOW = 4, 2, 128
B_GROUPS = ((128, 1), (512, 4), (2048, 16))
B_HEADS_PER_GROUP = 2
C_Q_HEADS, C_KV_HEADS = 6, 2
GRID_W = 64
ROPE_THETA = 10000.0
N_BRANCH = 3
NORM_EPS = 1e-6
MASK_VALUE = -1e30
Q_SCALE = HEAD_DIM ** -0.5
LOG2_E = 1.4426950408889634
LN_2 = 0.6931471805599453

A_Q = A_Q_HEADS * HEAD_DIM
A_KV = A_KV_HEADS * HEAD_DIM
B_W = B_HEADS_PER_GROUP * len(B_GROUPS) * HEAD_DIM
C_Q = C_Q_HEADS * HEAD_DIM
C_KV = C_KV_HEADS * HEAD_DIM
A_WIDTH = A_Q + 2 * A_KV
B_GROUP_WIDTH = 3 * B_HEADS_PER_GROUP * HEAD_DIM
IN_WIDTH = A_WIDTH + 3 * B_W + C_Q + 2 * C_KV

VMEM_LIMIT_BYTES = 58 * 1024 * 1024

TOKEN_BLOCK = 512
FFN_TOKEN_BLOCK = 1024
FF_CHUNK = 512
PROJ_CHUNK = 512
PROJ_STAGE_SLOTS = 4
MIX_CHUNK = 512
MIX_TOKEN_BLOCK = 1024
OUT_TOKEN_BLOCK = 1024
BAND_ROWS = 512
BAND_ROWS_DILATED = 2048
FLASH_Q_ROWS = 256
FLASH_K_ROWS = 512
FLASH_UNROLL = 8


def _params(*sem):
    return pltpu.CompilerParams(dimension_semantics=sem, vmem_limit_bytes=VMEM_LIMIT_BYTES)


def _rms(x, g):
    return x * lax.rsqrt(jnp.mean(x * x, axis=-1, keepdims=True) + NORM_EPS) * g


def _ffn_kernel(x_ref, g_ref, wg_ref, wu_ref, wo_ref, gf_ref, o_ref, n_ref, *, final_norm):
    j = pl.program_id(1)

    @pl.when(j == 0)
    def _():
        x = x_ref[...]
        n_ref[...] = _rms(x, g_ref[...]).astype(BF16)
        o_ref[...] = x

    n = n_ref[...]
    gate = jnp.dot(n, wg_ref[...], preferred_element_type=F32)
    up = jnp.dot(n, wu_ref[...], preferred_element_type=F32)
    a = (gate * jax.nn.sigmoid(gate) * (0.5 * up)).astype(BF16)
    o_ref[...] += jnp.dot(a, wo_ref[...], preferred_element_type=F32)

    if final_norm:
        @pl.when(j == pl.num_programs(1) - 1)
        def _():
            o_ref[...] = _rms(o_ref[...], gf_ref[...])


def _ffn(x, g, w_in, w_out, g_final, *, layer, final_norm):
    t, d = x.shape
    tm = min(FFN_TOKEN_BLOCK, t)
    nf = D_FF // FF_CHUNK
    return pl.pallas_call(
        functools.partial(_ffn_kernel, final_norm=final_norm),
        grid=(t // tm, nf),
        in_specs=[
            pl.BlockSpec((tm, d), lambda i, j: (i, 0)),
            pl.BlockSpec((1, d), lambda i, j: (0, 0)),
            pl.BlockSpec((None, d, FF_CHUNK), lambda i, j: (layer, 0, j)),
            pl.BlockSpec((None, d, FF_CHUNK), lambda i, j: (layer, 0, j + nf)),
            pl.BlockSpec((None, FF_CHUNK, d), lambda i, j: (layer, j, 0)),
            pl.BlockSpec((1, d), lambda i, j: (0, 0)),
        ],
        out_specs=pl.BlockSpec((tm, d), lambda i, j: (i, 0)),
        out_shape=jax.ShapeDtypeStruct((t, d), F32),
        scratch_shapes=[pltpu.VMEM((tm, d), BF16)],
        compiler_params=_params("parallel", "arbitrary"),
        name="ffn_final" if final_norm else "ffn",
    )(x, g, w_in, w_in, w_out, g_final)


def _rope(y, cos, sin):
    return y * cos + pltpu.roll(y, HEAD_DIM // 2, 1) * sin


def _axial_rope(y, cos, sin_lo, sin_hi):
    q = HEAD_DIM // 4
    return y * cos + pltpu.roll(y, HEAD_DIM - q, 1) * sin_lo + pltpu.roll(y, q, 1) * sin_hi


def _proj_layout():
    heads = []
    hpg = B_HEADS_PER_GROUP
    for h in range(A_Q_HEADS):
        heads.append(("rope_q", 0, h * HEAD_DIM))
    for h in range(A_KV_HEADS):
        heads.append(("rope", 0, A_Q + h * HEAD_DIM))
    for h in range(A_KV_HEADS):
        heads.append(("plain", 0, A_Q + A_KV + h * HEAD_DIM))
    for part, kind in enumerate(("rope_q", "rope", "plain")):
        for h in range(hpg * len(B_GROUPS)):
            heads.append((kind, 1 + h // hpg, (part * hpg + h % hpg) * HEAD_DIM))
    for h in range(C_Q_HEADS):
        heads.append(("axial_q", 4, h * HEAD_DIM))
    for h in range(C_KV_HEADS):
        heads.append(("axial_k", 5, h * HEAD_DIM))
    for h in range(C_KV_HEADS):
        heads.append(("plain", 5, C_KV + h * HEAD_DIM))
    assert len(heads) * HEAD_DIM == IN_WIDTH
    return heads


def _proj_kernel(h_ref, g_ref, w_ref, gq_ref, gk_ref, cos_ref, sin_ref, cosc_ref, sinlo_ref, sinhi_ref,
                 a_ref, b0_ref, b1_ref, b2_ref, cq_ref, ckv_ref, u_ref, stage_ref):
    outs = (a_ref, b0_ref, b1_ref, b2_ref, cq_ref, ckv_ref)
    u = _rms(h_ref[...], g_ref[...]).astype(BF16)
    u_ref[...] = u
    layout = _proj_layout()
    heads_per_chunk = PROJ_CHUNK // HEAD_DIM
    staged = 0
    nchunk = IN_WIDTH // PROJ_CHUNK
    cost = {"plain": 0, "rope": 1, "rope_q": 1, "axial_q": 2, "axial_k": 2}
    order = sorted(range(nchunk), key=lambda c: -sum(
        cost[layout[c * heads_per_chunk + hh][0]] for hh in range(heads_per_chunk)))
    for c in order:
        p = jnp.dot(u, w_ref[:, c * PROJ_CHUNK:(c + 1) * PROJ_CHUNK], preferred_element_type=F32)
        for hh in range(heads_per_chunk):
            kind, oi, col = layout[c * heads_per_chunk + hh]
            y = p[:, hh * HEAD_DIM:(hh + 1) * HEAD_DIM]
            if kind in ("rope", "rope_q"):
                y = _rope(y, cos_ref[...], sin_ref[...])
            elif kind in ("axial_q", "axial_k"):
                y = _rms(y, gq_ref[...] if kind == "axial_q" else gk_ref[...])
                y = _axial_rope(y, cosc_ref[...], sinlo_ref[...], sinhi_ref[...])
            if kind.endswith("_q"):
                y = y * (Q_SCALE * LOG2_E)
            dil = B_GROUPS[oi - 1][1] if 1 <= oi <= len(B_GROUPS) else 1
            if dil == 1:
                outs[oi][:, col:col + HEAD_DIM] = y.astype(BF16)
            else:
                slot = staged % PROJ_STAGE_SLOTS
                staged += 1
                stage_ref[slot] = y
                n = y.shape[0] // dil
                for r in range(dil):
                    outs[oi][r, :, col:col + HEAD_DIM] = stage_ref[slot, pl.ds(r, n, stride=dil), :].astype(BF16)


def _proj(h, g, w, gq, gk, tables, bsz, seq, layer):
    t, d = h.shape
    tm = min(TOKEN_BLOCK, seq)
    nseq = seq // tm
    row = lambda i: (i, 0)
    const = lambda i: (0, 0)
    pos = lambda i: (i % nseq, 0)
    out_specs, out_shape = [], []
    for width, dil in ((A_WIDTH, 1),) + tuple((B_GROUP_WIDTH, dl) for _, dl in B_GROUPS) + ((C_Q, 1), (2 * C_KV, 1)):
        if dil == 1:
            out_specs.append(pl.BlockSpec((tm, width), row))
            out_shape.append(jax.ShapeDtypeStruct((t, width), BF16))
        else:
            out_specs.append(pl.BlockSpec((None, dil, tm // dil, width), lambda i: (i // nseq, 0, i % nseq, 0)))
            out_shape.append(jax.ShapeDtypeStruct((bsz, dil, seq // dil, width), BF16))
    out_specs.append(pl.BlockSpec((tm, d), row))
    out_shape.append(jax.ShapeDtypeStruct((t, d), BF16))
    return pl.pallas_call(
        _proj_kernel,
        grid=(t // tm,),
        in_specs=[
            pl.BlockSpec((tm, d), row),
            pl.BlockSpec((1, d), const),
            pl.BlockSpec((None, d, IN_WIDTH), lambda i: (layer, 0, 0), pipeline_mode=pl.Buffered(1)),
            pl.BlockSpec((1, HEAD_DIM), const),
            pl.BlockSpec((1, HEAD_DIM), const),
        ] + [pl.BlockSpec((tm, HEAD_DIM), pos)] * 5,
        out_specs=out_specs,
        out_shape=out_shape,
        scratch_shapes=[pltpu.VMEM((PROJ_STAGE_SLOTS, tm, HEAD_DIM), F32)],
        compiler_params=_params("parallel"),
        name="proj",
    )(h, g, w, gq, gk, *tables)


def _band_kernel(*refs, W, G, HKV, TQ, L, has_sink, want_lse):
    refs = list(refs)
    main_ref, prev_ref, next_ref = refs[:3]
    refs = refs[3:]
    sink_ref = refs.pop(0) if has_sink else None
    o_ref = refs.pop(0)
    lse_ref = refs.pop(0) if want_lse else None

    i = pl.program_id(2)
    QB = min(HEAD_DIM, TQ)
    KW = QB + 2 * W
    nb = TQ // QB

    def head(ref, idx):
        return ref[:, idx * HEAD_DIM:(idx + 1) * HEAD_DIM]

    qs, ks, vs = [], [], []
    for h in range(HKV):
        kh, vh = HKV * G + h, HKV * G + HKV + h
        kcat = jnp.concatenate([head(prev_ref, kh), head(main_ref, kh), head(next_ref, kh)], axis=0)
        vcat = jnp.concatenate([head(prev_ref, vh), head(main_ref, vh), head(next_ref, vh)], axis=0)
        for b in range(nb):
            qg = [main_ref[b * QB:(b + 1) * QB, (h * G + g) * HEAD_DIM:(h * G + g + 1) * HEAD_DIM]
                  for g in range(G)]
            qs.append(jnp.concatenate(qg, axis=0) if G > 1 else qg[0])
            ks.append(kcat[b * QB:b * QB + KW])
            vs.append(vcat[b * QB:b * QB + KW])
    q3, k3, v3 = jnp.stack(qs), jnp.stack(ks), jnp.stack(vs)

    qrow = lax.broadcasted_iota(jnp.int32, (1, G * QB, KW), 1) % QB
    kcol = lax.broadcasted_iota(jnp.int32, (1, G * QB, KW), 2)
    in_band = jnp.abs(kcol - W - qrow) <= W
    blk = lax.broadcasted_iota(jnp.int32, (HKV * nb, 1, KW), 0) % nb
    kpos = lax.broadcasted_iota(jnp.int32, (HKV * nb, 1, KW), 2) + blk * QB + (i * TQ - W)
    end_bias = jnp.where(jnp.logical_and(kpos >= 0, kpos < L), 0.0, MASK_VALUE)

    s = jnp.einsum("bqd,bkd->bqk", q3, k3, preferred_element_type=F32)
    s = jnp.where(in_band, s, MASK_VALUE) + end_bias
    m = jnp.max(s, axis=-1, keepdims=True)
    p = jnp.exp2(s - m)
    l = jnp.sum(p, axis=-1, keepdims=True)
    if has_sink:
        sink = jnp.concatenate([
            jnp.concatenate([jnp.full((nb, QB, 1), sink_ref[h * G + g] * LOG2_E, F32) for g in range(G)], axis=1)
            for h in range(HKV)], axis=0)
        l = l + jnp.exp2(sink - m)
    o = jnp.einsum("bqk,bkd->bqd", p.astype(BF16), v3, preferred_element_type=F32) / l
    if want_lse:
        lse = jnp.broadcast_to(m * LN_2 + jnp.log(l), o.shape)
    for h in range(HKV):
        for b in range(nb):
            for g in range(G):
                col = (h * G + g) * HEAD_DIM
                dst = (slice(b * QB, (b + 1) * QB), slice(col, col + HEAD_DIM))
                o_ref[dst] = o[h * nb + b, g * QB:(g + 1) * QB].astype(o_ref.dtype)
                if want_lse:
                    lse_ref[dst] = lse[h * nb + b, g * QB:(g + 1) * QB]


def _band(x, sink, *, W, G, HKV, rows, want_lse):
    bsz, R, L, cw = x.shape
    assert cw == (HKV * G + 2 * HKV) * HEAD_DIM
    ow = HKV * G * HEAD_DIM
    tq = min(rows, L)
    nblk, per = L // tq, tq // W
    has_sink = sink is not None
    in_specs = [
        pl.BlockSpec((None, None, tq, cw), lambda b, r, i: (b, r, i, 0)),
        pl.BlockSpec((None, None, W, cw), lambda b, r, i: (b, r, jnp.maximum(i * per - 1, 0), 0)),
        pl.BlockSpec((None, None, W, cw), lambda b, r, i: (b, r, jnp.minimum((i + 1) * per, L // W - 1), 0)),
    ]
    args = [x, x, x]
    if has_sink:
        in_specs.append(pl.BlockSpec(memory_space=pltpu.SMEM))
        args.append(sink)
    out_spec = pl.BlockSpec((None, None, tq, ow), lambda b, r, i: (b, r, i, 0))
    out_specs, out_shape = [out_spec], [jax.ShapeDtypeStruct((bsz, R, L, ow), BF16)]
    if want_lse:
        out_specs.append(out_spec)
        out_shape.append(jax.ShapeDtypeStruct((bsz, R, L, ow), F32))
    return pl.pallas_call(
        functools.partial(_band_kernel, W=W, G=G, HKV=HKV, TQ=tq, L=L, has_sink=has_sink, want_lse=want_lse),
        grid=(bsz, R, nblk),
        in_specs=in_specs,
        out_specs=out_specs,
        out_shape=out_shape,
        compiler_params=_params("parallel", "parallel", "parallel"),
        name="band_sink" if has_sink else f"band_dil{R}",
    )(*args)


ONES_ROWS = 16


def _flash_kernel(q_ref, qn_ref, k_ref, v_ref, o_ref, qs_ref, vt_ref, s_ref, m_ref, acc_ref, *, TQ, TK, G, S):
    nchunks = S // TK
    ntrips = nchunks // FLASH_UNROLL
    first_block = pl.program_id(2) == 0

    def rows(chunk):
        return pl.ds(pl.multiple_of(chunk * TK, TK), TK)

    @pl.when(first_block)
    def _():
        vt_ref[HEAD_DIM:, :] = jnp.ones((ONES_ROWS, S), BF16)

        def fill(c, carry):
            vt_ref[:HEAD_DIM, rows(c)] = v_ref[0, rows(c), :].astype(F32).T.astype(BF16)
            return carry

        lax.fori_loop(0, nchunks, fill, 0)

    for g in range(G):
        qs_ref[0, g * TQ:(g + 1) * TQ, :] = q_ref[0, :, g * HEAD_DIM:(g + 1) * HEAD_DIM]
        qs_ref[1, g * TQ:(g + 1) * TQ, :] = qn_ref[0, :, g * HEAD_DIM:(g + 1) * HEAD_DIM]
    m_ref[...] = jnp.full(m_ref.shape, MASK_VALUE, F32)
    acc_ref[...] = jnp.zeros(acc_ref.shape, F32)

    def scores(chunk, slot, which=0):
        s_ref[slot] = lax.dot_general(k_ref[0, rows(chunk), :], qs_ref[which], (((1,), (1,)), ((), ())),
                                      preferred_element_type=F32)

    def accumulate(chunk, slot):
        s = s_ref[slot]
        m_prev = m_ref[...]
        m_next = jnp.maximum(m_prev, jnp.max(s, axis=0, keepdims=True))
        alpha = jnp.exp2(m_prev - m_next)
        p = jnp.exp2(s - m_next).astype(BF16)
        acc_ref[...] = alpha * acc_ref[...] + jnp.dot(vt_ref[:, rows(chunk)], p, preferred_element_type=F32)
        m_ref[...] = m_next

    @pl.when(first_block)
    def _():
        scores(0, 0)

    def step(j, carry):
        for u in range(FLASH_UNROLL):
            c = FLASH_UNROLL * j + u
            if u < FLASH_UNROLL - 1:
                scores(c + 1, (u + 1) % 2)
            else:
                handoff = (j == ntrips - 1).astype(jnp.int32)
                scores((c + 1) * (1 - handoff), 0, handoff)
            accumulate(c, u % 2)
        return carry

    lax.fori_loop(0, ntrips, step, 0)
    out = (acc_ref[:HEAD_DIM, :] / acc_ref[HEAD_DIM:HEAD_DIM + 1, :]).T
    for g in range(G):
        o_ref[0, :, g * HEAD_DIM:(g + 1) * HEAD_DIM] = out[g * TQ:(g + 1) * TQ].astype(BF16)


def _flash(q, kv):
    bsz, S, _ = q.shape
    G = C_Q_HEADS // C_KV_HEADS
    tq = min(FLASH_Q_ROWS, S)
    tk = min(FLASH_K_ROWS, S // FLASH_UNROLL)
    assert S % (tk * FLASH_UNROLL) == 0 and FLASH_UNROLL % 2 == 0
    gw = G * HEAD_DIM
    return pl.pallas_call(
        functools.partial(_flash_kernel, TQ=tq, TK=tk, G=G, S=S),
        grid=(bsz, C_KV_HEADS, S // tq),
        in_specs=[
            pl.BlockSpec((1, tq, gw), lambda b, h, i: (b, i, h)),
            pl.BlockSpec((1, tq, gw), lambda b, h, i: (b, jnp.minimum(i + 1, S // tq - 1), h)),
            pl.BlockSpec((1, S, HEAD_DIM), lambda b, h, i: (b, 0, h)),
            pl.BlockSpec((1, S, HEAD_DIM), lambda b, h, i: (b, 0, C_KV_HEADS + h)),
        ],
        out_specs=pl.BlockSpec((1, tq, gw), lambda b, h, i: (b, i, h)),
        out_shape=jax.ShapeDtypeStruct((bsz, S, C_Q), BF16),
        scratch_shapes=[
            pltpu.VMEM((2, G * tq, HEAD_DIM), BF16),
            pltpu.VMEM((HEAD_DIM + ONES_ROWS, S), BF16),
            pltpu.VMEM((2, tk, G * tq), F32),
            pltpu.VMEM((1, G * tq), F32),
            pltpu.VMEM((HEAD_DIM + ONES_ROWS, G * tq), F32),
        ],
        compiler_params=_params("parallel", "parallel", "arbitrary"),
        name="flash",
    )(q, q, kv, kv)


def _mix_kernel(u_ref, ya_ref, ob0_ref, ob1_ref, ob2_ref, ls0_ref, ls1_ref, ls2_ref, yc_ref,
                wg0_ref, wg1_ref, wg2_ref, bg0_ref, bg1_ref, bg2_ref, wba_ref, wbb_ref, wbc_ref,
                o_ref, yb_ref, tok_ref):
    c = pl.program_id(1)

    @pl.when(c == 0)
    def _():
        obs = (ob0_ref, ob1_ref, ob2_ref)
        lss = (ls0_ref, ls1_ref, ls2_ref)
        gw = B_HEADS_PER_GROUP * HEAD_DIM
        slot = 0

        def token_major(ref, sl, dil, slot):
            if dil == 1:
                return ref[:, sl].astype(F32)
            n = ref.shape[1]
            for r in range(dil):
                tok_ref[slot, pl.ds(r, n, stride=dil), :] = ref[r, :, sl].astype(F32)
            return tok_ref[slot]

        for j in range(B_HEADS_PER_GROUP):
            sl = slice(j * HEAD_DIM, (j + 1) * HEAD_DIM)
            ls, ob = [], []
            for gi, (_, dil) in enumerate(B_GROUPS):
                ls.append(token_major(lss[gi], sl, dil, slot))
                ob.append(token_major(obs[gi], sl, dil, slot + 1))
                slot += 2 if dil > 1 else 0
            m = jnp.maximum(jnp.maximum(ls[0], ls[1]), ls[2])
            es = [jnp.exp(v - m) for v in ls]
            den = es[0] + es[1] + es[2]
            for gi in range(len(B_GROUPS)):
                wt = es[gi] / den
                yb_ref[:, gi * gw + j * HEAD_DIM:gi * gw + (j + 1) * HEAD_DIM] = (ob[gi] * wt).astype(BF16)

    u = u_ref[...]

    def gate(w_ref, b_ref):
        return jax.nn.sigmoid(jnp.dot(u, w_ref[...], preferred_element_type=F32) + b_ref[...])

    merged = gate(wg0_ref, bg0_ref) * jnp.dot(ya_ref[...], wba_ref[...], preferred_element_type=F32)
    merged += gate(wg1_ref, bg1_ref) * jnp.dot(yb_ref[...], wbb_ref[...], preferred_element_type=F32)
    merged += gate(wg2_ref, bg2_ref) * jnp.dot(yc_ref[...], wbc_ref[...], preferred_element_type=F32)
    o_ref[...] = merged.astype(BF16)


def _mix(u, ya, obs, lss, yc, w_gate, b_gate, wb_a, wb_b, wb_c, seq, layer):
    t, d = u.shape
    tm = min(MIX_TOKEN_BLOCK, seq)
    nseq = seq // tm
    nc = d // MIX_CHUNK
    row = lambda i, c: (i, 0)
    gw = B_HEADS_PER_GROUP * HEAD_DIM
    group_specs = [
        pl.BlockSpec((tm, gw), row) if dil == 1 else
        pl.BlockSpec((None, dil, tm // dil, gw), lambda i, c: (i // nseq, 0, i % nseq, 0))
        for _, dil in B_GROUPS]
    n_stage = 2 * B_HEADS_PER_GROUP * sum(dil > 1 for _, dil in B_GROUPS)
    gate_spec = lambda k: pl.BlockSpec((None, d, MIX_CHUNK), lambda i, c: (layer, 0, k * nc + c))
    bias_spec = lambda k: pl.BlockSpec((1, MIX_CHUNK), lambda i, c: (0, k * nc + c))
    col_spec = lambda rows: pl.BlockSpec((None, rows, MIX_CHUNK), lambda i, c: (layer, 0, c))
    return pl.pallas_call(
        _mix_kernel,
        grid=(t // tm, nc),
        in_specs=[
            pl.BlockSpec((tm, d), row),
            pl.BlockSpec((tm, A_Q), row),
            *group_specs, *group_specs,
            pl.BlockSpec((tm, C_Q), row),
            gate_spec(0), gate_spec(1), gate_spec(2),
            bias_spec(0), bias_spec(1), bias_spec(2),
            col_spec(A_Q), col_spec(B_W), col_spec(C_Q),
        ],
        out_specs=pl.BlockSpec((tm, MIX_CHUNK), lambda i, c: (i, c)),
        out_shape=jax.ShapeDtypeStruct((t, d), BF16),
        scratch_shapes=[pltpu.VMEM((tm, B_W), BF16), pltpu.VMEM((n_stage, tm, HEAD_DIM), F32)],
        compiler_params=_params("parallel", "arbitrary"),
        name="mix",
    )(u, ya, *obs, *lss, yc, w_gate, w_gate, w_gate, b_gate, b_gate, b_gate, wb_a, wb_b, wb_c)


def _out_proj_kernel(h_ref, m_ref, wo_ref, o_ref):
    o_ref[...] = h_ref[...] + jnp.dot(m_ref[...], wo_ref[...], preferred_element_type=F32)


def _out_proj(h, merged, w_o, layer):
    t, d = h.shape
    tm = min(OUT_TOKEN_BLOCK, t)
    row = lambda i: (i, 0)
    return pl.pallas_call(
        _out_proj_kernel,
        grid=(t // tm,),
        in_specs=[
            pl.BlockSpec((tm, d), row),
            pl.BlockSpec((tm, d), row),
            pl.BlockSpec((None, d, d), lambda i: (layer, 0, 0), pipeline_mode=pl.Buffered(1)),
        ],
        out_specs=pl.BlockSpec((tm, d), row),
        out_shape=jax.ShapeDtypeStruct((t, d), F32),
        compiler_params=_params("parallel"),
        name="out_proj",
    )(h, merged, w_o)


def _rope_tables(seq):
    def angles(pos, dim):
        inv = ROPE_THETA ** (-jnp.arange(0, dim, 2, dtype=F32) / dim)
        return pos.astype(F32)[:, None] * inv[None, :]

    pos = jnp.arange(seq)
    ang = angles(pos, HEAD_DIM)
    cos, sin = jnp.cos(ang), jnp.sin(ang)
    ar = angles(pos // GRID_W, HEAD_DIM // 2)
    ac = angles(pos % GRID_W, HEAD_DIM // 2)
    cr, sr, cc, sc = jnp.cos(ar), jnp.sin(ar), jnp.cos(ac), jnp.sin(ac)
    z = jnp.zeros_like(sr)
    return (
        jnp.concatenate([cos, cos], -1),
        jnp.concatenate([-sin, sin], -1),
        jnp.concatenate([cr, cr, cc, cc], -1),
        jnp.concatenate([-sr, z, -sc, z], -1),
        jnp.concatenate([z, sr, z, sc], -1),
    )


def _prepare_weights(g_ffn1, w_ffn1_in, w_ffn1_out, g_mix, w_in, a_sink, c_q_norm, c_k_norm,
                     w_branch, w_gate, b_gate, w_o, g_ffn2, w_ffn2_in, w_ffn2_out):
    bf = lambda w: w.astype(BF16)
    depth = g_ffn1.shape[0]
    rows = lambda v: [v[l].reshape(1, -1) for l in range(depth)]
    return dict(
        g_ffn1=rows(g_ffn1), w_ffn1_in=bf(w_ffn1_in), w_ffn1_out=bf(w_ffn1_out),
        g_mix=rows(g_mix), w_in=bf(w_in), a_sink=[a_sink[l].reshape(-1) for l in range(depth)],
        c_q_norm=rows(c_q_norm), c_k_norm=rows(c_k_norm),
        wb_a=bf(w_branch[:, :A_Q]), wb_b=bf(w_branch[:, A_Q:A_Q + B_W]), wb_c=bf(w_branch[:, A_Q + B_W:]),
        w_gate=bf(w_gate), b_gate=rows(b_gate), w_o=bf(w_o),
        g_ffn2=rows(g_ffn2), w_ffn2_in=bf(w_ffn2_in), w_ffn2_out=bf(w_ffn2_out),
    )


def _trunk(x, w, g_final, tables):
    bsz, seq, d = x.shape
    t = bsz * seq
    depth = len(w["g_ffn1"])
    gw = B_HEADS_PER_GROUP * HEAD_DIM
    x = x.reshape(t, d)
    for l in range(depth):
        h = _ffn(x, w["g_ffn1"][l], w["w_ffn1_in"], w["w_ffn1_out"], g_final, layer=l, final_norm=False)
        a, b0, b1, b2, cq, ckv, u = _proj(h, w["g_mix"][l], w["w_in"], w["c_q_norm"][l], w["c_k_norm"][l],
                                          tables, bsz, seq, l)
        (ya,) = _band(a.reshape(bsz, 1, seq, A_WIDTH), w["a_sink"][l], W=A_HALF_WINDOW,
                      G=A_Q_HEADS // A_KV_HEADS, HKV=A_KV_HEADS, rows=BAND_ROWS, want_lse=False)
        obs, lss = [], []
        for bg, (window, dil) in zip((b0, b1, b2), B_GROUPS):
            o, ls = _band(bg.reshape(bsz, dil, seq // dil, B_GROUP_WIDTH), None, W=window // (2 * dil),
                          G=1, HKV=B_HEADS_PER_GROUP, rows=BAND_ROWS_DILATED, want_lse=True)
            obs.append(o.reshape(t, gw) if dil == 1 else o)
            lss.append(ls.reshape(t, gw) if dil == 1 else ls)
        yc = _flash(cq.reshape(bsz, seq, C_Q), ckv.reshape(bsz, seq, 2 * C_KV))
        merged = _mix(u, ya.reshape(t, A_Q), obs, lss, yc.reshape(t, C_Q),
                      w["w_gate"], w["b_gate"][l], w["wb_a"], w["wb_b"], w["wb_c"], seq, l)
        h = _out_proj(h, merged, w["w_o"], l)
        x = _ffn(h, w["g_ffn2"][l], w["w_ffn2_in"], w["w_ffn2_out"], g_final, layer=l,
                 final_norm=(l == depth - 1))
    return x.reshape(bsz, seq, d)


def kernel(x_prompt, x_sample, g_ffn1, w_ffn1_in, w_ffn1_out, g_mix, w_in, a_sink, c_q_norm, c_k_norm,
           w_branch, w_gate, b_gate, w_o, g_ffn2, w_ffn2_in, w_ffn2_out, g_final):
    w = _prepare_weights(g_ffn1, w_ffn1_in, w_ffn1_out, g_mix, w_in, a_sink, c_q_norm, c_k_norm,
                         w_branch, w_gate, b_gate, w_o, g_ffn2, w_ffn2_in, w_ffn2_out)
    gf = g_final.reshape(1, -1)
    tables = _rope_tables(max(x_prompt.shape[1], x_sample.shape[1]))
    return _trunk(x_prompt, w, gf, tables), _trunk(x_sample, w, gf, tables)
```

```python
import functools

import jax
import jax.numpy as jnp
from jax import lax
from jax.experimental import pallas as pl
from jax.experimental.pallas import tpu as pltpu

F32 = jnp.float32
BF16 = jnp.bfloat16

D_MODEL = 2048
D_FF = 5632
HEAD_DIM = 128
A_Q_HEADS, A_KV_HEADS, A_HALF_WINDOW = 4, 2, 128
B_GROUPS = ((128, 1), (512, 4), (2048, 16))
B_HEADS_PER_GROUP = 2
C_Q_HEADS, C_KV_HEADS = 6, 2
GRID_W = 64
ROPE_THETA = 10000.0
N_BRANCH = 3
NORM_EPS = 1e-6
MASK_VALUE = -1e30
Q_SCALE = HEAD_DIM ** -0.5
LOG2_E = 1.4426950408889634
LN_2 = 0.6931471805599453

A_Q = A_Q_HEADS * HEAD_DIM
A_KV = A_KV_HEADS * HEAD_DIM
B_W = B_HEADS_PER_GROUP * len(B_GROUPS) * HEAD_DIM
C_Q = C_Q_HEADS * HEAD_DIM
C_KV = C_KV_HEADS * HEAD_DIM
A_WIDTH = A_Q + 2 * A_KV
B_GROUP_WIDTH = 3 * B_HEADS_PER_GROUP * HEAD_DIM
IN_WIDTH = A_WIDTH + 3 * B_W + C_Q + 2 * C_KV

VMEM_LIMIT_BYTES = 58 * 1024 * 1024

TOKEN_BLOCK = 512
FFN_TOKEN_BLOCK = 1024
FF_CHUNK = 512
PROJ_CHUNK = 512
PROJ_STAGE_SLOTS = 4
MIX_CHUNK = 512
MIX_TOKEN_BLOCK = 1024
OUT_TOKEN_BLOCK = 1024
BAND_ROWS = 512
BAND_ROWS_DILATED = 2048
FLASH_Q_ROWS = 256
FLASH_K_ROWS = 512
FLASH_MAX_UNROLL = 16


def _params(*sem):
    return pltpu.CompilerParams(dimension_semantics=sem, vmem_limit_bytes=VMEM_LIMIT_BYTES)


def _rms(x, g):
    return x * lax.rsqrt(jnp.mean(x * x, axis=-1, keepdims=True) + NORM_EPS) * g


def _ffn_kernel(x_ref, g_ref, wg_ref, wu_ref, wo_ref, gf_ref, o_ref, n_ref, *, final_norm):
    j = pl.program_id(1)

    @pl.when(j == 0)
    def _():
        x = x_ref[...]
        n_ref[...] = _rms(x, g_ref[...]).astype(BF16)
        o_ref[...] = x

    n = n_ref[...]
    gate = jnp.dot(n, wg_ref[...], preferred_element_type=F32)
    up = jnp.dot(n, wu_ref[...], preferred_element_type=F32)
    a = (gate * jax.nn.sigmoid(gate) * (0.5 * up)).astype(BF16)
    o_ref[...] += jnp.dot(a, wo_ref[...], preferred_element_type=F32)

    if final_norm:
        @pl.when(j == pl.num_programs(1) - 1)
        def _():
            o_ref[...] = _rms(o_ref[...], gf_ref[...])


def _ffn(x, g, w_in, w_out, g_final, *, layer, final_norm):
    t, d = x.shape
    tm = min(FFN_TOKEN_BLOCK, t)
    nf = D_FF // FF_CHUNK
    return pl.pallas_call(
        functools.partial(_ffn_kernel, final_norm=final_norm),
        grid=(t // tm, nf),
        in_specs=[
            pl.BlockSpec((tm, d), lambda i, j: (i, 0)),
            pl.BlockSpec((1, d), lambda i, j: (0, 0)),
            pl.BlockSpec((None, d, FF_CHUNK), lambda i, j: (layer, 0, j)),
            pl.BlockSpec((None, d, FF_CHUNK), lambda i, j: (layer, 0, j + nf)),
            pl.BlockSpec((None, FF_CHUNK, d), lambda i, j: (layer, j, 0)),
            pl.BlockSpec((1, d), lambda i, j: (0, 0)),
        ],
        out_specs=pl.BlockSpec((tm, d), lambda i, j: (i, 0)),
        out_shape=jax.ShapeDtypeStruct((t, d), F32),
        scratch_shapes=[pltpu.VMEM((tm, d), BF16)],
        compiler_params=_params("parallel", "arbitrary"),
        name="ffn_final" if final_norm else "ffn",
    )(x, g, w_in, w_in, w_out, g_final)


def _rope(y, cos, sin):
    return y * cos + pltpu.roll(y, HEAD_DIM // 2, 1) * sin


def _axial_rope(y, cos, sin_lo, sin_hi):
    q = HEAD_DIM // 4
    return y * cos + pltpu.roll(y, HEAD_DIM - q, 1) * sin_lo + pltpu.roll(y, q, 1) * sin_hi


def _proj_layout():
    heads = []
    hpg = B_HEADS_PER_GROUP
    for h in range(A_Q_HEADS):
        heads.append(("rope_q", 0, h * HEAD_DIM))
    for h in range(A_KV_HEADS):
        heads.append(("rope", 0, A_Q + h * HEAD_DIM))
    for h in range(A_KV_HEADS):
        heads.append(("plain", 0, A_Q + A_KV + h * HEAD_DIM))
    for part, kind in enumerate(("rope_q", "rope", "plain")):
        for h in range(hpg * len(B_GROUPS)):
            heads.append((kind, 1 + h // hpg, (part * hpg + h % hpg) * HEAD_DIM))
    for h in range(C_Q_HEADS):
        heads.append(("axial_q", 4, h * HEAD_DIM))
    for h in range(C_KV_HEADS):
        heads.append(("axial_k", 5, h * HEAD_DIM))
    for h in range(C_KV_HEADS):
        heads.append(("plain", 5, C_KV + h * HEAD_DIM))
    assert len(heads) * HEAD_DIM == IN_WIDTH
    return heads


def _proj_kernel(h_ref, g_ref, w_ref, gq_ref, gk_ref, cos_ref, sin_ref, cosc_ref, sinlo_ref, sinhi_ref,
                 a_ref, b0_ref, b1_ref, b2_ref, cq_ref, ckv_ref, u_ref, stage_ref):
    outs = (a_ref, b0_ref, b1_ref, b2_ref, cq_ref, ckv_ref)
    u = _rms(h_ref[...], g_ref[...]).astype(BF16)
    u_ref[...] = u
    layout = _proj_layout()
    heads_per_chunk = PROJ_CHUNK // HEAD_DIM
    staged = 0
    nchunk = IN_WIDTH // PROJ_CHUNK
    cost = {"plain": 0, "rope": 1, "rope_q": 1, "axial_q": 2, "axial_k": 2}
    order = sorted(range(nchunk), key=lambda c: -sum(
        cost[layout[c * heads_per_chunk + hh][0]] for hh in range(heads_per_chunk)))
    for c in order:
        p = jnp.dot(u, w_ref[:, c * PROJ_CHUNK:(c + 1) * PROJ_CHUNK], preferred_element_type=F32)
        for hh in range(heads_per_chunk):
            kind, oi, col = layout[c * heads_per_chunk + hh]
            y = p[:, hh * HEAD_DIM:(hh + 1) * HEAD_DIM]
            if kind in ("rope", "rope_q"):
                y = _rope(y, cos_ref[...], sin_ref[...])
            elif kind in ("axial_q", "axial_k"):
                y = _rms(y, gq_ref[...] if kind == "axial_q" else gk_ref[...])
                y = _axial_rope(y, cosc_ref[...], sinlo_ref[...], sinhi_ref[...])
            if kind.endswith("_q"):
                y = y * (Q_SCALE * LOG2_E)
            dil = B_GROUPS[oi - 1][1] if 1 <= oi <= len(B_GROUPS) else 1
            if dil == 1:
                outs[oi][:, col:col + HEAD_DIM] = y.astype(BF16)
            else:
                slot = staged % PROJ_STAGE_SLOTS
                staged += 1
                stage_ref[slot] = y
                n = y.shape[0] // dil
                for r in range(dil):
                    outs[oi][r, :, col:col + HEAD_DIM] = stage_ref[slot, pl.ds(r, n, stride=dil), :].astype(BF16)


def _proj(h, g, w, gq, gk, tables, bsz, seq, layer):
    t, d = h.shape
    tm = min(TOKEN_BLOCK, seq)
    nseq = seq // tm
    row = lambda i: (i, 0)
    const = lambda i: (0, 0)
    pos = lambda i: (i % nseq, 0)
    out_specs, out_shape = [], []
    for width, dil in ((A_WIDTH, 1),) + tuple((B_GROUP_WIDTH, dl) for _, dl in B_GROUPS) + ((C_Q, 1), (2 * C_KV, 1)):
        if dil == 1:
            out_specs.append(pl.BlockSpec((tm, width), row))
            out_shape.append(jax.ShapeDtypeStruct((t, width), BF16))
        else:
            out_specs.append(pl.BlockSpec((None, dil, tm // dil, width), lambda i: (i // nseq, 0, i % nseq, 0)))
            out_shape.append(jax.ShapeDtypeStruct((bsz, dil, seq // dil, width), BF16))
    out_specs.append(pl.BlockSpec((tm, d), row))
    out_shape.append(jax.ShapeDtypeStruct((t, d), BF16))
    return pl.pallas_call(
        _proj_kernel,
        grid=(t // tm,),
        in_specs=[
            pl.BlockSpec((tm, d), row),
            pl.BlockSpec((1, d), const),
            pl.BlockSpec((None, d, IN_WIDTH), lambda i: (layer, 0, 0), pipeline_mode=pl.Buffered(1)),
            pl.BlockSpec((1, HEAD_DIM), const),
            pl.BlockSpec((1, HEAD_DIM), const),
        ] + [pl.BlockSpec((tm, HEAD_DIM), pos)] * 5,
        out_specs=out_specs,
        out_shape=out_shape,
        scratch_shapes=[pltpu.VMEM((PROJ_STAGE_SLOTS, tm, HEAD_DIM), F32)],
        compiler_params=_params("parallel"),
        name="proj",
    )(h, g, w, gq, gk, *tables)


def _band_kernel(*refs, W, G, HKV, TQ, L, has_sink, want_lse):
    refs = list(refs)
    main_ref, prev_ref, next_ref = refs[:3]
    refs = refs[3:]
    sink_ref = refs.pop(0) if has_sink else None
    o_ref = refs.pop(0)
    lse_ref = refs.pop(0) if want_lse else None

    i = pl.program_id(2)
    QB = min(HEAD_DIM, TQ)
    KW = QB + 2 * W
    nb = TQ // QB

    def head(ref, idx):
        return ref[:, idx * HEAD_DIM:(idx + 1) * HEAD_DIM]

    qs, ks, vs = [], [], []
    for h in range(HKV):
        kh, vh = HKV * G + h, HKV * G + HKV + h
        kcat = jnp.concatenate([head(prev_ref, kh), head(main_ref, kh), head(next_ref, kh)], axis=0)
        vcat = jnp.concatenate([head(prev_ref, vh), head(main_ref, vh), head(next_ref, vh)], axis=0)
        for b in range(nb):
            qg = [main_ref[b * QB:(b + 1) * QB, (h * G + g) * HEAD_DIM:(h * G + g + 1) * HEAD_DIM]
                  for g in range(G)]
            qs.append(jnp.concatenate(qg, axis=0) if G > 1 else qg[0])
            ks.append(kcat[b * QB:b * QB + KW])
            vs.append(vcat[b * QB:b * QB + KW])
    q3, k3, v3 = jnp.stack(qs), jnp.stack(ks), jnp.stack(vs)

    qrow = lax.broadcasted_iota(jnp.int32, (1, G * QB, KW), 1) % QB
    kcol = lax.broadcasted_iota(jnp.int32, (1, G * QB, KW), 2)
    in_band = jnp.abs(kcol - W - qrow) <= W
    blk = lax.broadcasted_iota(jnp.int32, (HKV * nb, 1, KW), 0) % nb
    kpos = lax.broadcasted_iota(jnp.int32, (HKV * nb, 1, KW), 2) + blk * QB + (i * TQ - W)
    end_bias = jnp.where(jnp.logical_and(kpos >= 0, kpos < L), 0.0, MASK_VALUE)

    s = jnp.einsum("bqd,bkd->bqk", q3, k3, preferred_element_type=F32)
    s = jnp.where(in_band, s, MASK_VALUE) + end_bias
    m = jnp.max(s, axis=-1, keepdims=True)
    p = jnp.exp2(s - m)
    l = jnp.sum(p, axis=-1, keepdims=True)
    if has_sink:
        sink = jnp.concatenate([
            jnp.concatenate([jnp.full((nb, QB, 1), sink_ref[h * G + g] * LOG2_E, F32) for g in range(G)], axis=1)
            for h in range(HKV)], axis=0)
        l = l + jnp.exp2(sink - m)
    o = jnp.einsum("bqk,bkd->bqd", p.astype(BF16), v3, preferred_element_type=F32) / l
    if want_lse:
        lse = jnp.broadcast_to(m * LN_2 + jnp.log(l), o.shape)
    for h in range(HKV):
        for b in range(nb):
            for g in range(G):
                col = (h * G + g) * HEAD_DIM
                dst = (slice(b * QB, (b + 1) * QB), slice(col, col + HEAD_DIM))
                o_ref[dst] = o[h * nb + b, g * QB:(g + 1) * QB].astype(o_ref.dtype)
                if want_lse:
                    lse_ref[dst] = lse[h * nb + b, g * QB:(g + 1) * QB]


def _band(x, sink, *, W, G, HKV, rows, want_lse):
    bsz, R, L, cw = x.shape
    assert cw == (HKV * G + 2 * HKV) * HEAD_DIM
    ow = HKV * G * HEAD_DIM
    tq = min(rows, L)
    nblk, per = L // tq, tq // W
    has_sink = sink is not None
    in_specs = [
        pl.BlockSpec((None, None, tq, cw), lambda b, r, i: (b, r, i, 0)),
        pl.BlockSpec((None, None, W, cw), lambda b, r, i: (b, r, jnp.maximum(i * per - 1, 0), 0)),
        pl.BlockSpec((None, None, W, cw), lambda b, r, i: (b, r, jnp.minimum((i + 1) * per, L // W - 1), 0)),
    ]
    args = [x, x, x]
    if has_sink:
        in_specs.append(pl.BlockSpec(memory_space=pltpu.SMEM))
        args.append(sink)
    out_spec = pl.BlockSpec((None, None, tq, ow), lambda b, r, i: (b, r, i, 0))
    out_specs, out_shape = [out_spec], [jax.ShapeDtypeStruct((bsz, R, L, ow), BF16)]
    if want_lse:
        out_specs.append(out_spec)
        out_shape.append(jax.ShapeDtypeStruct((bsz, R, L, ow), F32))
    return pl.pallas_call(
        functools.partial(_band_kernel, W=W, G=G, HKV=HKV, TQ=tq, L=L, has_sink=has_sink, want_lse=want_lse),
        grid=(bsz, R, nblk),
        in_specs=in_specs,
        out_specs=out_specs,
        out_shape=out_shape,
        compiler_params=_params("parallel", "parallel", "parallel"),
        name="band_sink" if has_sink else f"band_dil{R}",
    )(*args)


ONES_ROWS = 16


def _flash_kernel(q_ref, qn_ref, k_ref, v_ref, o_ref, qs_ref, vt_ref, s_ref, m_ref, acc_ref,
                  *, TQ, TK, G, S, UNROLL):
    nchunks = S // TK
    ntrips = nchunks // UNROLL
    first_block = pl.program_id(2) == 0

    def rows(chunk):
        return pl.ds(pl.multiple_of(chunk * TK, TK), TK)

    @pl.when(first_block)
    def _():
        vt_ref[HEAD_DIM:, :] = jnp.ones((ONES_ROWS, S), BF16)

        def fill(c, carry):
            vt_ref[:HEAD_DIM, rows(c)] = v_ref[0, rows(c), :].astype(F32).T.astype(BF16)
            return carry

        lax.fori_loop(0, nchunks, fill, 0)

    for g in range(G):
        qs_ref[0, g * TQ:(g + 1) * TQ, :] = q_ref[0, :, g * HEAD_DIM:(g + 1) * HEAD_DIM]
        qs_ref[1, g * TQ:(g + 1) * TQ, :] = qn_ref[0, :, g * HEAD_DIM:(g + 1) * HEAD_DIM]
    m_ref[...] = jnp.full(m_ref.shape, MASK_VALUE, F32)
    acc_ref[...] = jnp.zeros(acc_ref.shape, F32)

    def scores(chunk, slot, which=0):
        s_ref[slot] = lax.dot_general(k_ref[0, rows(chunk), :], qs_ref[which], (((1,), (1,)), ((), ())),
                                      preferred_element_type=F32)

    def accumulate(chunk, slot):
        s = s_ref[slot]
        m_prev = m_ref[...]
        m_next = jnp.maximum(m_prev, jnp.max(s, axis=0, keepdims=True))
        alpha = jnp.exp2(m_prev - m_next)
        p = jnp.exp2(s - m_next).astype(BF16)
        acc_ref[...] = alpha * acc_ref[...] + jnp.dot(vt_ref[:, rows(chunk)], p, preferred_element_type=F32)
        m_ref[...] = m_next

    @pl.when(first_block)
    def _():
        scores(0, 0)

    def step(j, carry):
        for u in range(UNROLL):
            c = UNROLL * j + u
            if u < UNROLL - 1:
                scores(c + 1, (u + 1) % 2)
            else:
                handoff = jnp.asarray(j == ntrips - 1, dtype=jnp.int32)
                scores((c + 1) * (1 - handoff), 0, handoff)
            accumulate(c, u % 2)
        return carry

    lax.fori_loop(0, ntrips, step, 0)
    out = (acc_ref[:HEAD_DIM, :] / acc_ref[HEAD_DIM:HEAD_DIM + 1, :]).T
    for g in range(G):
        o_ref[0, :, g * HEAD_DIM:(g + 1) * HEAD_DIM] = out[g * TQ:(g + 1) * TQ].astype(BF16)


def _flash(q, kv):
    bsz, S, _ = q.shape
    G = C_Q_HEADS // C_KV_HEADS
    tq = min(FLASH_Q_ROWS, S)
    tk = min(FLASH_K_ROWS, S // 2)
    nchunks = S // tk
    unroll = max([u for u in (2, 4, 8, FLASH_MAX_UNROLL) if nchunks % u == 0 and nchunks // u >= 2] or [2])
    assert S % tk == 0 and nchunks % unroll == 0
    gw = G * HEAD_DIM
    return pl.pallas_call(
        functools.partial(_flash_kernel, TQ=tq, TK=tk, G=G, S=S, UNROLL=unroll),
        grid=(bsz, C_KV_HEADS, S // tq),
        in_specs=[
            pl.BlockSpec((1, tq, gw), lambda b, h, i: (b, i, h)),
            pl.BlockSpec((1, tq, gw), lambda b, h, i: (b, jnp.minimum(i + 1, S // tq - 1), h)),
            pl.BlockSpec((1, S, HEAD_DIM), lambda b, h, i: (b, 0, h)),
            pl.BlockSpec((1, S, HEAD_DIM), lambda b, h, i: (b, 0, C_KV_HEADS + h)),
        ],
        out_specs=pl.BlockSpec((1, tq, gw), lambda b, h, i: (b, i, h)),
        out_shape=jax.ShapeDtypeStruct((bsz, S, C_Q), BF16),
        scratch_shapes=[
            pltpu.VMEM((2, G * tq, HEAD_DIM), BF16),
            pltpu.VMEM((HEAD_DIM + ONES_ROWS, S), BF16),
            pltpu.VMEM((2, tk, G * tq), F32),
            pltpu.VMEM((1, G * tq), F32),
            pltpu.VMEM((HEAD_DIM + ONES_ROWS, G * tq), F32),
        ],
        compiler_params=_params("parallel", "parallel", "arbitrary"),
        name="flash",
    )(q, q, kv, kv)


def _mix_kernel(u_ref, ya_ref, ob0_ref, ob1_ref, ob2_ref, ls0_ref, ls1_ref, ls2_ref, yc_ref,
                wg0_ref, wg1_ref, wg2_ref, bg0_ref, bg1_ref, bg2_ref, wba_ref, wbb_ref, wbc_ref,
                o_ref, yb_ref, tok_ref):
    c = pl.program_id(1)

    @pl.when(c == 0)
    def _():
        obs = (ob0_ref, ob1_ref, ob2_ref)
        lss = (ls0_ref, ls1_ref, ls2_ref)
        gw = B_HEADS_PER_GROUP * HEAD_DIM
        slot = 0

        def token_major(ref, sl, dil, slot):
            if dil == 1:
                return ref[:, sl].astype(F32)
            n = ref.shape[1]
            for r in range(dil):
                tok_ref[slot, pl.ds(r, n, stride=dil), :] = ref[r, :, sl].astype(F32)
            return tok_ref[slot]

        for j in range(B_HEADS_PER_GROUP):
            sl = slice(j * HEAD_DIM, (j + 1) * HEAD_DIM)
            ls, ob = [], []
            for gi, (_, dil) in enumerate(B_GROUPS):
                ls.append(token_major(lss[gi], sl, dil, slot))
                ob.append(token_major(obs[gi], sl, dil, slot + 1))
                slot += 2 if dil > 1 else 0
            m = jnp.maximum(jnp.maximum(ls[0], ls[1]), ls[2])
            es = [jnp.exp(v - m) for v in ls]
            den = es[0] + es[1] + es[2]
            for gi in range(len(B_GROUPS)):
                wt = es[gi] / den
                yb_ref[:, gi * gw + j * HEAD_DIM:gi * gw + (j + 1) * HEAD_DIM] = (ob[gi] * wt).astype(BF16)

    u = u_ref[...]

    def gate(w_ref, b_ref):
        return jax.nn.sigmoid(jnp.dot(u, w_ref[...], preferred_element_type=F32) + b_ref[...])

    merged = gate(wg0_ref, bg0_ref) * jnp.dot(ya_ref[...], wba_ref[...], preferred_element_type=F32)
    merged += gate(wg1_ref, bg1_ref) * jnp.dot(yb_ref[...], wbb_ref[...], preferred_element_type=F32)
    merged += gate(wg2_ref, bg2_ref) * jnp.dot(yc_ref[...], wbc_ref[...], preferred_element_type=F32)
    o_ref[...] = merged.astype(BF16)


def _mix(u, ya, obs, lss, yc, w_gate, b_gate, wb_a, wb_b, wb_c, seq, layer):
    t, d = u.shape
    tm = min(MIX_TOKEN_BLOCK, seq)
    nseq = seq // tm
    nc = d // MIX_CHUNK
    row = lambda i, c: (i, 0)
    gw = B_HEADS_PER_GROUP * HEAD_DIM
    group_specs = [
        pl.BlockSpec((tm, gw), row) if dil == 1 else
        pl.BlockSpec((None, dil, tm // dil, gw), lambda i, c: (i // nseq, 0, i % nseq, 0))
        for _, dil in B_GROUPS]
    n_stage = 2 * B_HEADS_PER_GROUP * sum(dil > 1 for _, dil in B_GROUPS)
    gate_spec = lambda k: pl.BlockSpec((None, d, MIX_CHUNK), lambda i, c: (layer, 0, k * nc + c))
    bias_spec = lambda k: pl.BlockSpec((1, MIX_CHUNK), lambda i, c: (0, k * nc + c))
    col_spec = lambda rows: pl.BlockSpec((None, rows, MIX_CHUNK), lambda i, c: (layer, 0, c))
    return pl.pallas_call(
        _mix_kernel,
        grid=(t // tm, nc),
        in_specs=[
            pl.BlockSpec((tm, d), row),
            pl.BlockSpec((tm, A_Q), row),
            *group_specs, *group_specs,
            pl.BlockSpec((tm, C_Q), row),
            gate_spec(0), gate_spec(1), gate_spec(2),
            bias_spec(0), bias_spec(1), bias_spec(2),
            col_spec(A_Q), col_spec(B_W), col_spec(C_Q),
        ],
        out_specs=pl.BlockSpec((tm, MIX_CHUNK), lambda i, c: (i, c)),
        out_shape=jax.ShapeDtypeStruct((t, d), BF16),
        scratch_shapes=[pltpu.VMEM((tm, B_W), BF16), pltpu.VMEM((n_stage, tm, HEAD_DIM), F32)],
        compiler_params=_params("parallel", "arbitrary"),
        name="mix",
    )(u, ya, *obs, *lss, yc, w_gate, w_gate, w_gate, b_gate, b_gate, b_gate, wb_a, wb_b, wb_c)


def _out_proj_kernel(h_ref, m_ref, wo_ref, o_ref):
    o_ref[...] = h_ref[...] + jnp.dot(m_ref[...], wo_ref[...], preferred_element_type=F32)


def _out_proj(h, merged, w_o, layer):
    t, d = h.shape
    tm = min(OUT_TOKEN_BLOCK, t)
    row = lambda i: (i, 0)
    return pl.pallas_call(
        _out_proj_kernel,
        grid=(t // tm,),
        in_specs=[
            pl.BlockSpec((tm, d), row),
            pl.BlockSpec((tm, d), row),
            pl.BlockSpec((None, d, d), lambda i: (layer, 0, 0), pipeline_mode=pl.Buffered(1)),
        ],
        out_specs=pl.BlockSpec((tm, d), row),
        out_shape=jax.ShapeDtypeStruct((t, d), F32),
        compiler_params=_params("parallel"),
        name="out_proj",
    )(h, merged, w_o)


def _rope_tables(seq):
    def angles(pos, dim):
        inv = ROPE_THETA ** (-jnp.arange(0, dim, 2, dtype=F32) / dim)
        return pos.astype(F32)[:, None] * inv[None, :]

    pos = jnp.arange(seq)
    ang = angles(pos, HEAD_DIM)
    cos, sin = jnp.cos(ang), jnp.sin(ang)
    ar = angles(pos // GRID_W, HEAD_DIM // 2)
    ac = angles(pos % GRID_W, HEAD_DIM // 2)
    cr, sr, cc, sc = jnp.cos(ar), jnp.sin(ar), jnp.cos(ac), jnp.sin(ac)
    z = jnp.zeros_like(sr)
    return (
        jnp.concatenate([cos, cos], -1),
        jnp.concatenate([-sin, sin], -1),
        jnp.concatenate([cr, cr, cc, cc], -1),
        jnp.concatenate([-sr, z, -sc, z], -1),
        jnp.concatenate([z, sr, z, sc], -1),
    )


def _prepare_weights(g_ffn1, w_ffn1_in, w_ffn1_out, g_mix, w_in, a_sink, c_q_norm, c_k_norm,
                     w_branch, w_gate, b_gate, w_o, g_ffn2, w_ffn2_in, w_ffn2_out):
    bf = lambda w: w.astype(BF16)
    depth = g_ffn1.shape[0]
    rows = lambda v: [v[l].reshape(1, -1) for l in range(depth)]
    return dict(
        g_ffn1=rows(g_ffn1), w_ffn1_in=bf(w_ffn1_in), w_ffn1_out=bf(w_ffn1_out),
        g_mix=rows(g_mix), w_in=bf(w_in), a_sink=[a_sink[l].reshape(-1) for l in range(depth)],
        c_q_norm=rows(c_q_norm), c_k_norm=rows(c_k_norm),
        wb_a=bf(w_branch[:, :A_Q]), wb_b=bf(w_branch[:, A_Q:A_Q + B_W]), wb_c=bf(w_branch[:, A_Q + B_W:]),
        w_gate=bf(w_gate), b_gate=rows(b_gate), w_o=bf(w_o),
        g_ffn2=rows(g_ffn2), w_ffn2_in=bf(w_ffn2_in), w_ffn2_out=bf(w_ffn2_out),
    )


def _trunk(x, w, g_final, tables):
    bsz, seq, d = x.shape
    t = bsz * seq
    depth = len(w["g_ffn1"])
    gw = B_HEADS_PER_GROUP * HEAD_DIM
    x = x.reshape(t, d)
    for l in range(depth):
        h = _ffn(x, w["g_ffn1"][l], w["w_ffn1_in"], w["w_ffn1_out"], g_final, layer=l, final_norm=False)
        a, b0, b1, b2, cq, ckv, u = _proj(h, w["g_mix"][l], w["w_in"], w["c_q_norm"][l], w["c_k_norm"][l],
                                          tables, bsz, seq, l)
        (ya,) = _band(a.reshape(bsz, 1, seq, A_WIDTH), w["a_sink"][l], W=A_HALF_WINDOW,
                      G=A_Q_HEADS // A_KV_HEADS, HKV=A_KV_HEADS, rows=BAND_ROWS, want_lse=False)
        obs, lss = [], []
        for bg, (window, dil) in zip((b0, b1, b2), B_GROUPS):
            o, ls = _band(bg.reshape(bsz, dil, seq // dil, B_GROUP_WIDTH), None, W=window // (2 * dil),
                          G=1, HKV=B_HEADS_PER_GROUP, rows=BAND_ROWS_DILATED, want_lse=True)
            obs.append(o.reshape(t, gw) if dil == 1 else o)
            lss.append(ls.reshape(t, gw) if dil == 1 else ls)
        yc = _flash(cq.reshape(bsz, seq, C_Q), ckv.reshape(bsz, seq, 2 * C_KV))
        merged = _mix(u, ya.reshape(t, A_Q), obs, lss, yc.reshape(t, C_Q),
                      w["w_gate"], w["b_gate"][l], w["wb_a"], w["wb_b"], w["wb_c"], seq, l)
        h = _out_proj(h, merged, w["w_o"], l)
        x = _ffn(h, w["g_ffn2"][l], w["w_ffn2_in"], w["w_ffn2_out"], g_final, layer=l,
                 final_norm=(l == depth - 1))
    return x.reshape(bsz, seq, d)


def kernel(x_prompt, x_sample, g_ffn1, w_ffn1_in, w_ffn1_out, g_mix, w_in, a_sink, c_q_norm, c_k_norm,
           w_branch, w_gate, b_gate, w_o, g_ffn2, w_ffn2_in, w_ffn2_out, g_final):
    w = _prepare_weights(g_ffn1, w_ffn1_in, w_ffn1_out, g_mix, w_in, a_sink, c_q_norm, c_k_norm,
                         w_branch, w_gate, b_gate, w_o, g_ffn2, w_ffn2_in, w_ffn2_out)
    gf = g_final.reshape(1, -1)
    tables = _rope_tables(max(x_prompt.shape[1], x_sample.shape[1]))
    return _trunk(x_prompt, w, gf, tables), _trunk(x_sample, w, gf, tables)
```

```python
import functools

import jax
import jax.numpy as jnp
from jax import lax
from jax.experimental import pallas as pl
from jax.experimental.pallas import tpu as pltpu

F32 = jnp.float32
BF16 = jnp.bfloat16

D_MODEL = 2048
D_FF = 5632
HEAD_DIM = 128
A_Q_HEADS, A_KV_HEADS, A_HALF_WINDOW = 4, 2, 128
B_GROUPS = ((128, 1), (512, 4), (2048, 16))
B_HEADS_PER_GROUP = 2
C_Q_HEADS, C_KV_HEADS = 6, 2
GRID_W = 64
ROPE_THETA = 10000.0
N_BRANCH = 3
NORM_EPS = 1e-6
MASK_VALUE = -1e30
Q_SCALE = HEAD_DIM ** -0.5
LOG2_E = 1.4426950408889634
LN_2 = 0.6931471805599453

A_Q = A_Q_HEADS * HEAD_DIM
A_KV = A_KV_HEADS * HEAD_DIM
B_W = B_HEADS_PER_GROUP * len(B_GROUPS) * HEAD_DIM
C_Q = C_Q_HEADS * HEAD_DIM
C_KV = C_KV_HEADS * HEAD_DIM
A_WIDTH = A_Q + 2 * A_KV
B_GROUP_WIDTH = 3 * B_HEADS_PER_GROUP * HEAD_DIM
IN_WIDTH = A_WIDTH + 3 * B_W + C_Q + 2 * C_KV

VMEM_LIMIT_BYTES = 58 * 1024 * 1024

TOKEN_BLOCK = 512
FFN_TOKEN_BLOCK = 1024
FF_CHUNK = 512
PROJ_CHUNK = 512
PROJ_STAGE_SLOTS = 4
MIX_CHUNK = 512
MIX_TOKEN_BLOCK = 1024
OUT_TOKEN_BLOCK = 1024
BAND_ROWS = 2048
BAND_ROWS_DILATED = 4096
FLASH_Q_ROWS = 256
FLASH_K_ROWS = 512
FLASH_MAX_UNROLL = 16


def _params(*sem):
    return pltpu.CompilerParams(dimension_semantics=sem, vmem_limit_bytes=VMEM_LIMIT_BYTES)


def _rms(x, g):
    return x * lax.rsqrt(jnp.mean(x * x, axis=-1, keepdims=True) + NORM_EPS) * g


def _ffn_kernel(x_ref, g_ref, wg_ref, wu_ref, wo_ref, gf_ref, o_ref, n_ref, *, final_norm):
    j = pl.program_id(1)

    @pl.when(j == 0)
    def _():
        x = x_ref[...]
        n_ref[...] = _rms(x, g_ref[...]).astype(BF16)
        o_ref[...] = x

    n = n_ref[...]
    gate = jnp.dot(n, wg_ref[...], preferred_element_type=F32)
    up = jnp.dot(n, wu_ref[...], preferred_element_type=F32)
    a = (gate * jax.nn.sigmoid(gate) * (0.5 * up)).astype(BF16)
    o_ref[...] += jnp.dot(a, wo_ref[...], preferred_element_type=F32)

    if final_norm:
        @pl.when(j == pl.num_programs(1) - 1)
        def _():
            o_ref[...] = _rms(o_ref[...], gf_ref[...])


def _ffn(x, g, w_in, w_out, g_final, *, layer, final_norm):
    t, d = x.shape
    tm = min(FFN_TOKEN_BLOCK, t)
    nf = D_FF // FF_CHUNK
    return pl.pallas_call(
        functools.partial(_ffn_kernel, final_norm=final_norm),
        grid=(t // tm, nf),
        in_specs=[
            pl.BlockSpec((tm, d), lambda i, j: (i, 0)),
            pl.BlockSpec((1, d), lambda i, j: (0, 0)),
            pl.BlockSpec((None, d, FF_CHUNK), lambda i, j: (layer, 0, j)),
            pl.BlockSpec((None, d, FF_CHUNK), lambda i, j: (layer, 0, j + nf)),
            pl.BlockSpec((None, FF_CHUNK, d), lambda i, j: (layer, j, 0)),
            pl.BlockSpec((1, d), lambda i, j: (0, 0)),
        ],
        out_specs=pl.BlockSpec((tm, d), lambda i, j: (i, 0)),
        out_shape=jax.ShapeDtypeStruct((t, d), F32),
        scratch_shapes=[pltpu.VMEM((tm, d), BF16)],
        compiler_params=_params("parallel", "arbitrary"),
        name="ffn_final" if final_norm else "ffn",
    )(x, g, w_in, w_in, w_out, g_final)


def _rope(y, cos, sin):
    return y * cos + pltpu.roll(y, HEAD_DIM // 2, 1) * sin


def _axial_rope(y, cos, sin_lo, sin_hi):
    q = HEAD_DIM // 4
    return y * cos + pltpu.roll(y, HEAD_DIM - q, 1) * sin_lo + pltpu.roll(y, q, 1) * sin_hi


def _proj_layout():
    heads = []
    hpg = B_HEADS_PER_GROUP
    for h in range(A_Q_HEADS):
        heads.append(("rope_q", 0, h * HEAD_DIM))
    for h in range(A_KV_HEADS):
        heads.append(("rope", 0, A_Q + h * HEAD_DIM))
    for h in range(A_KV_HEADS):
        heads.append(("plain", 0, A_Q + A_KV + h * HEAD_DIM))
    for part, kind in enumerate(("rope_q", "rope", "plain")):
        for h in range(hpg * len(B_GROUPS)):
            heads.append((kind, 1 + h // hpg, (part * hpg + h % hpg) * HEAD_DIM))
    for h in range(C_Q_HEADS):
        heads.append(("axial_q", 4, h * HEAD_DIM))
    for h in range(C_KV_HEADS):
        heads.append(("axial_k", 5, h * HEAD_DIM))
    for h in range(C_KV_HEADS):
        heads.append(("plain", 5, C_KV + h * HEAD_DIM))
    assert len(heads) * HEAD_DIM == IN_WIDTH
    return heads


def _proj_kernel(h_ref, g_ref, w_ref, gq_ref, gk_ref, cos_ref, sin_ref, cosc_ref, sinlo_ref, sinhi_ref,
                 a_ref, b0_ref, b1_ref, b2_ref, cq_ref, ckv_ref, u_ref, stage_ref):
    outs = (a_ref, b0_ref, b1_ref, b2_ref, cq_ref, ckv_ref)
    u = _rms(h_ref[...], g_ref[...]).astype(BF16)
    u_ref[...] = u
    layout = _proj_layout()
    heads_per_chunk = PROJ_CHUNK // HEAD_DIM
    staged = 0
    nchunk = IN_WIDTH // PROJ_CHUNK
    cost = {"plain": 0, "rope": 1, "rope_q": 1, "axial_q": 2, "axial_k": 2}
    order = sorted(range(nchunk), key=lambda c: -sum(
        cost[layout[c * heads_per_chunk + hh][0]] for hh in range(heads_per_chunk)))
    for c in order:
        p = jnp.dot(u, w_ref[:, c * PROJ_CHUNK:(c + 1) * PROJ_CHUNK], preferred_element_type=F32)
        for hh in range(heads_per_chunk):
            kind, oi, col = layout[c * heads_per_chunk + hh]
            y = p[:, hh * HEAD_DIM:(hh + 1) * HEAD_DIM]
            if kind in ("rope", "rope_q"):
                y = _rope(y, cos_ref[...], sin_ref[...])
            elif kind in ("axial_q", "axial_k"):
                y = _rms(y, gq_ref[...] if kind == "axial_q" else gk_ref[...])
                y = _axial_rope(y, cosc_ref[...], sinlo_ref[...], sinhi_ref[...])
            if kind.endswith("_q"):
                y = y * (Q_SCALE * LOG2_E)
            dil = B_GROUPS[oi - 1][1] if 1 <= oi <= len(B_GROUPS) else 1
            if dil == 1:
                outs[oi][:, col:col + HEAD_DIM] = y.astype(BF16)
            else:
                slot = staged % PROJ_STAGE_SLOTS
                staged += 1
                stage_ref[slot] = y
                n = y.shape[0] // dil
                for r in range(dil):
                    outs[oi][r, :, col:col + HEAD_DIM] = stage_ref[slot, pl.ds(r, n, stride=dil), :].astype(BF16)


def _proj(h, g, w, gq, gk, tables, bsz, seq, layer):
    t, d = h.shape
    tm = min(TOKEN_BLOCK, seq)
    nseq = seq // tm
    row = lambda i: (i, 0)
    const = lambda i: (0, 0)
    pos = lambda i: (i % nseq, 0)
    out_specs, out_shape = [], []
    for width, dil in ((A_WIDTH, 1),) + tuple((B_GROUP_WIDTH, dl) for _, dl in B_GROUPS) + ((C_Q, 1), (2 * C_KV, 1)):
        if dil == 1:
            out_specs.append(pl.BlockSpec((tm, width), row))
            out_shape.append(jax.ShapeDtypeStruct((t, width), BF16))
        else:
            out_specs.append(pl.BlockSpec((None, dil, tm // dil, width), lambda i: (i // nseq, 0, i % nseq, 0)))
            out_shape.append(jax.ShapeDtypeStruct((bsz, dil, seq // dil, width), BF16))
    out_specs.append(pl.BlockSpec((tm, d), row))
    out_shape.append(jax.ShapeDtypeStruct((t, d), BF16))
    return pl.pallas_call(
        _proj_kernel,
        grid=(t // tm,),
        in_specs=[
            pl.BlockSpec((tm, d), row),
            pl.BlockSpec((1, d), const),
            pl.BlockSpec((None, d, IN_WIDTH), lambda i: (layer, 0, 0), pipeline_mode=pl.Buffered(1)),
            pl.BlockSpec((1, HEAD_DIM), const),
            pl.BlockSpec((1, HEAD_DIM), const),
        ] + [pl.BlockSpec((tm, HEAD_DIM), pos)] * 5,
        out_specs=out_specs,
        out_shape=out_shape,
        scratch_shapes=[pltpu.VMEM((PROJ_STAGE_SLOTS, tm, HEAD_DIM), F32)],
        compiler_params=_params("parallel"),
        name="proj",
    )(h, g, w, gq, gk, *tables)


def _band_kernel(*refs, W, G, HKV, TQ, L, has_sink, want_lse):
    refs = list(refs)
    main_ref, prev_ref, next_ref = refs[:3]
    refs = refs[3:]
    sink_ref = refs.pop(0) if has_sink else None
    o_ref = refs.pop(0)
    lse_ref = refs.pop(0) if want_lse else None

    i = pl.program_id(2)
    QB = min(HEAD_DIM, TQ)
    KW = QB + 2 * W
    nb = TQ // QB

    def head(ref, idx):
        return ref[:, idx * HEAD_DIM:(idx + 1) * HEAD_DIM]

    qs, ks, vs = [], [], []
    for h in range(HKV):
        kh, vh = HKV * G + h, HKV * G + HKV + h
        kcat = jnp.concatenate([head(prev_ref, kh), head(main_ref, kh), head(next_ref, kh)], axis=0)
        vcat = jnp.concatenate([head(prev_ref, vh), head(main_ref, vh), head(next_ref, vh)], axis=0)
        for b in range(nb):
            qg = [main_ref[b * QB:(b + 1) * QB, (h * G + g) * HEAD_DIM:(h * G + g + 1) * HEAD_DIM]
                  for g in range(G)]
            qs.append(jnp.concatenate(qg, axis=0) if G > 1 else qg[0])
            ks.append(kcat[b * QB:b * QB + KW])
            vs.append(vcat[b * QB:b * QB + KW])
    q3, k3, v3 = jnp.stack(qs), jnp.stack(ks), jnp.stack(vs)

    qrow = lax.broadcasted_iota(jnp.int32, (1, G * QB, KW), 1) % QB
    kcol = lax.broadcasted_iota(jnp.int32, (1, G * QB, KW), 2)
    in_band = jnp.abs(kcol - W - qrow) <= W
    blk = lax.broadcasted_iota(jnp.int32, (HKV * nb, 1, KW), 0) % nb
    kpos = lax.broadcasted_iota(jnp.int32, (HKV * nb, 1, KW), 2) + blk * QB + (i * TQ - W)
    end_bias = jnp.where(jnp.logical_and(kpos >= 0, kpos < L), 0.0, MASK_VALUE)

    s = jnp.einsum("bqd,bkd->bqk", q3, k3, preferred_element_type=F32)
    s = jnp.where(in_band, s, MASK_VALUE) + end_bias
    m = jnp.max(s, axis=-1, keepdims=True)
    p = jnp.exp2(s - m)
    l = jnp.sum(p, axis=-1, keepdims=True)
    if has_sink:
        sink = jnp.concatenate([
            jnp.concatenate([jnp.full((nb, QB, 1), sink_ref[h * G + g] * LOG2_E, F32) for g in range(G)], axis=1)
            for h in range(HKV)], axis=0)
        l = l + jnp.exp2(sink - m)
    o = jnp.einsum("bqk,bkd->bqd", p.astype(BF16), v3, preferred_element_type=F32) / l
    if want_lse:
        lse = jnp.broadcast_to(m * LN_2 + jnp.log(l), o.shape)
    for h in range(HKV):
        for b in range(nb):
            for g in range(G):
                col = (h * G + g) * HEAD_DIM
                dst = (slice(b * QB, (b + 1) * QB), slice(col, col + HEAD_DIM))
                o_ref[dst] = o[h * nb + b, g * QB:(g + 1) * QB].astype(o_ref.dtype)
                if want_lse:
                    lse_ref[dst] = lse[h * nb + b, g * QB:(g + 1) * QB]


def _band(x, sink, *, W, G, HKV, rows, want_lse):
    bsz, R, L, cw = x.shape
    assert cw == (HKV * G + 2 * HKV) * HEAD_DIM
    ow = HKV * G * HEAD_DIM
    tq = min(rows, L)
    nblk, per = L // tq, tq // W
    has_sink = sink is not None
    in_specs = [
        pl.BlockSpec((None, None, tq, cw), lambda b, r, i: (b, r, i, 0)),
        pl.BlockSpec((None, None, W, cw), lambda b, r, i: (b, r, jnp.maximum(i * per - 1, 0), 0)),
        pl.BlockSpec((None, None, W, cw), lambda b, r, i: (b, r, jnp.minimum((i + 1) * per, L // W - 1), 0)),
    ]
    args = [x, x, x]
    if has_sink:
        in_specs.append(pl.BlockSpec(memory_space=pltpu.SMEM))
        args.append(sink)
    out_spec = pl.BlockSpec((None, None, tq, ow), lambda b, r, i: (b, r, i, 0))
    out_specs, out_shape = [out_spec], [jax.ShapeDtypeStruct((bsz, R, L, ow), BF16)]
    if want_lse:
        out_specs.append(out_spec)
        out_shape.append(jax.ShapeDtypeStruct((bsz, R, L, ow), F32))
    return pl.pallas_call(
        functools.partial(_band_kernel, W=W, G=G, HKV=HKV, TQ=tq, L=L, has_sink=has_sink, want_lse=want_lse),
        grid=(bsz, R, nblk),
        in_specs=in_specs,
        out_specs=out_specs,
        out_shape=out_shape,
        compiler_params=_params("parallel", "parallel", "parallel"),
        name="band_sink" if has_sink else f"band_dil{R}",
    )(*args)


ONES_ROWS = 16


def _flash_kernel(q_ref, qn_ref, k_ref, v_ref, o_ref, qs_ref, vt_ref, s_ref, m_ref, acc_ref,
                  *, TQ, TK, G, S, UNROLL):
    nchunks = S // TK
    ntrips = nchunks // UNROLL
    first_block = pl.program_id(2) == 0

    def rows(chunk):
        return pl.ds(pl.multiple_of(chunk * TK, TK), TK)

    @pl.when(first_block)
    def _():
        vt_ref[HEAD_DIM:, :] = jnp.ones((ONES_ROWS, S), BF16)

        def fill(c, carry):
            vt_ref[:HEAD_DIM, rows(c)] = v_ref[0, rows(c), :].astype(F32).T.astype(BF16)
            return carry

        lax.fori_loop(0, nchunks, fill, 0)

    for g in range(G):
        qs_ref[0, g * TQ:(g + 1) * TQ, :] = q_ref[0, :, g * HEAD_DIM:(g + 1) * HEAD_DIM]
        qs_ref[1, g * TQ:(g + 1) * TQ, :] = qn_ref[0, :, g * HEAD_DIM:(g + 1) * HEAD_DIM]
    m_ref[...] = jnp.full(m_ref.shape, MASK_VALUE, F32)
    acc_ref[...] = jnp.zeros(acc_ref.shape, F32)

    def scores(chunk, slot, which=0):
        s_ref[slot] = lax.dot_general(k_ref[0, rows(chunk), :], qs_ref[which], (((1,), (1,)), ((), ())),
                                      preferred_element_type=F32)

    def accumulate(chunk, slot):
        s = s_ref[slot]
        m_prev = m_ref[...]
        m_next = jnp.maximum(m_prev, jnp.max(s, axis=0, keepdims=True))
        alpha = jnp.exp2(m_prev - m_next)
        p = jnp.exp2(s - m_next).astype(BF16)
        acc_ref[...] = alpha * acc_ref[...] + jnp.dot(vt_ref[:, rows(chunk)], p, preferred_element_type=F32)
        m_ref[...] = m_next

    @pl.when(first_block)
    def _():
        scores(0, 0)

    def step(j, carry):
        for u in range(UNROLL):
            c = UNROLL * j + u
            if u < UNROLL - 1:
                scores(c + 1, (u + 1) % 2)
            else:
                handoff = jnp.asarray(j == ntrips - 1, dtype=jnp.int32)
                scores((c + 1) * (1 - handoff), 0, handoff)
            accumulate(c, u % 2)
        return carry

    lax.fori_loop(0, ntrips, step, 0)
    out = (acc_ref[:HEAD_DIM, :] / acc_ref[HEAD_DIM:HEAD_DIM + 1, :]).T
    for g in range(G):
        o_ref[0, :, g * HEAD_DIM:(g + 1) * HEAD_DIM] = out[g * TQ:(g + 1) * TQ].astype(BF16)


def _flash(q, kv):
    bsz, S, _ = q.shape
    G = C_Q_HEADS // C_KV_HEADS
    tq = min(FLASH_Q_ROWS, S)
    tk = min(FLASH_K_ROWS, S // 2)
    nchunks = S // tk
    unroll = max([u for u in (2, 4, 8, FLASH_MAX_UNROLL) if nchunks % u == 0 and nchunks // u >= 2] or [2])
    assert S % tk == 0 and nchunks % unroll == 0
    gw = G * HEAD_DIM
    return pl.pallas_call(
        functools.partial(_flash_kernel, TQ=tq, TK=tk, G=G, S=S, UNROLL=unroll),
        grid=(bsz, C_KV_HEADS, S // tq),
        in_specs=[
            pl.BlockSpec((1, tq, gw), lambda b, h, i: (b, i, h)),
            pl.BlockSpec((1, tq, gw), lambda b, h, i: (b, jnp.minimum(i + 1, S // tq - 1), h)),
            pl.BlockSpec((1, S, HEAD_DIM), lambda b, h, i: (b, 0, h)),
            pl.BlockSpec((1, S, HEAD_DIM), lambda b, h, i: (b, 0, C_KV_HEADS + h)),
        ],
        out_specs=pl.BlockSpec((1, tq, gw), lambda b, h, i: (b, i, h)),
        out_shape=jax.ShapeDtypeStruct((bsz, S, C_Q), BF16),
        scratch_shapes=[
            pltpu.VMEM((2, G * tq, HEAD_DIM), BF16),
            pltpu.VMEM((HEAD_DIM + ONES_ROWS, S), BF16),
            pltpu.VMEM((2, tk, G * tq), F32),
            pltpu.VMEM((1, G * tq), F32),
            pltpu.VMEM((HEAD_DIM + ONES_ROWS, G * tq), F32),
        ],
        compiler_params=_params("parallel", "parallel", "arbitrary"),
        name="flash",
    )(q, q, kv, kv)


def _mix_kernel(u_ref, ya_ref, ob0_ref, ob1_ref, ob2_ref, ls0_ref, ls1_ref, ls2_ref, yc_ref,
                wg0_ref, wg1_ref, wg2_ref, bg0_ref, bg1_ref, bg2_ref, wba_ref, wbb_ref, wbc_ref,
                o_ref, yb_ref, tok_ref):
    c = pl.program_id(1)

    @pl.when(c == 0)
    def _():
        obs = (ob0_ref, ob1_ref, ob2_ref)
        lss = (ls0_ref, ls1_ref, ls2_ref)
        gw = B_HEADS_PER_GROUP * HEAD_DIM
        slot = 0

        def token_major(ref, sl, dil, slot):
            if dil == 1:
                return ref[:, sl].astype(F32)
            n = ref.shape[1]
            for r in range(dil):
                tok_ref[slot, pl.ds(r, n, stride=dil), :] = ref[r, :, sl].astype(F32)
            return tok_ref[slot]

        for j in range(B_HEADS_PER_GROUP):
            sl = slice(j * HEAD_DIM, (j + 1) * HEAD_DIM)
            ls, ob = [], []
            for gi, (_, dil) in enumerate(B_GROUPS):
                ls.append(token_major(lss[gi], sl, dil, slot))
                ob.append(token_major(obs[gi], sl, dil, slot + 1))
                slot += 2 if dil > 1 else 0
            m = jnp.maximum(jnp.maximum(ls[0], ls[1]), ls[2])
            es = [jnp.exp(v - m) for v in ls]
            den = es[0] + es[1] + es[2]
            for gi in range(len(B_GROUPS)):
                wt = es[gi] / den
                yb_ref[:, gi * gw + j * HEAD_DIM:gi * gw + (j + 1) * HEAD_DIM] = (ob[gi] * wt).astype(BF16)

    u = u_ref[...]

    def gate(w_ref, b_ref):
        return jax.nn.sigmoid(jnp.dot(u, w_ref[...], preferred_element_type=F32) + b_ref[...])

    merged = gate(wg0_ref, bg0_ref) * jnp.dot(ya_ref[...], wba_ref[...], preferred_element_type=F32)
    merged += gate(wg1_ref, bg1_ref) * jnp.dot(yb_ref[...], wbb_ref[...], preferred_element_type=F32)
    merged += gate(wg2_ref, bg2_ref) * jnp.dot(yc_ref[...], wbc_ref[...], preferred_element_type=F32)
    o_ref[...] = merged.astype(BF16)


def _mix(u, ya, obs, lss, yc, w_gate, b_gate, wb_a, wb_b, wb_c, seq, layer):
    t, d = u.shape
    tm = min(MIX_TOKEN_BLOCK, seq)
    nseq = seq // tm
    nc = d // MIX_CHUNK
    row = lambda i, c: (i, 0)
    gw = B_HEADS_PER_GROUP * HEAD_DIM
    group_specs = [
        pl.BlockSpec((tm, gw), row) if dil == 1 else
        pl.BlockSpec((None, dil, tm // dil, gw), lambda i, c: (i // nseq, 0, i % nseq, 0))
        for _, dil in B_GROUPS]
    n_stage = 2 * B_HEADS_PER_GROUP * sum(dil > 1 for _, dil in B_GROUPS)
    gate_spec = lambda k: pl.BlockSpec((None, d, MIX_CHUNK), lambda i, c: (layer, 0, k * nc + c))
    bias_spec = lambda k: pl.BlockSpec((1, MIX_CHUNK), lambda i, c: (0, k * nc + c))
    col_spec = lambda rows: pl.BlockSpec((None, rows, MIX_CHUNK), lambda i, c: (layer, 0, c))
    return pl.pallas_call(
        _mix_kernel,
        grid=(t // tm, nc),
        in_specs=[
            pl.BlockSpec((tm, d), row),
            pl.BlockSpec((tm, A_Q), row),
            *group_specs, *group_specs,
            pl.BlockSpec((tm, C_Q), row),
            gate_spec(0), gate_spec(1), gate_spec(2),
            bias_spec(0), bias_spec(1), bias_spec(2),
            col_spec(A_Q), col_spec(B_W), col_spec(C_Q),
        ],
        out_specs=pl.BlockSpec((tm, MIX_CHUNK), lambda i, c: (i, c)),
        out_shape=jax.ShapeDtypeStruct((t, d), BF16),
        scratch_shapes=[pltpu.VMEM((tm, B_W), BF16), pltpu.VMEM((n_stage, tm, HEAD_DIM), F32)],
        compiler_params=_params("parallel", "arbitrary"),
        name="mix",
    )(u, ya, *obs, *lss, yc, w_gate, w_gate, w_gate, b_gate, b_gate, b_gate, wb_a, wb_b, wb_c)


def _out_proj_kernel(h_ref, m_ref, wo_ref, o_ref):
    o_ref[...] = h_ref[...] + jnp.dot(m_ref[...], wo_ref[...], preferred_element_type=F32)


def _out_proj(h, merged, w_o, layer):
    t, d = h.shape
    tm = min(OUT_TOKEN_BLOCK, t)
    row = lambda i: (i, 0)
    return pl.pallas_call(
        _out_proj_kernel,
        grid=(t // tm,),
        in_specs=[
            pl.BlockSpec((tm, d), row),
            pl.BlockSpec((tm, d), row),
            pl.BlockSpec((None, d, d), lambda i: (layer, 0, 0), pipeline_mode=pl.Buffered(1)),
        ],
        out_specs=pl.BlockSpec((tm, d), row),
        out_shape=jax.ShapeDtypeStruct((t, d), F32),
        compiler_params=_params("parallel"),
        name="out_proj",
    )(h, merged, w_o)


def _rope_tables(seq):
    def angles(pos, dim):
        inv = ROPE_THETA ** (-jnp.arange(0, dim, 2, dtype=F32) / dim)
        return pos.astype(F32)[:, None] * inv[None, :]

    pos = jnp.arange(seq)
    ang = angles(pos, HEAD_DIM)
    cos, sin = jnp.cos(ang), jnp.sin(ang)
    ar = angles(pos // GRID_W, HEAD_DIM // 2)
    ac = angles(pos % GRID_W, HEAD_DIM // 2)
    cr, sr, cc, sc = jnp.cos(ar), jnp.sin(ar), jnp.cos(ac), jnp.sin(ac)
    z = jnp.zeros_like(sr)
    return (
        jnp.concatenate([cos, cos], -1),
        jnp.concatenate([-sin, sin], -1),
        jnp.concatenate([cr, cr, cc, cc], -1),
        jnp.concatenate([-sr, z, -sc, z], -1),
        jnp.concatenate([z, sr, z, sc], -1),
    )


def _prepare_weights(g_ffn1, w_ffn1_in, w_ffn1_out, g_mix, w_in, a_sink, c_q_norm, c_k_norm,
                     w_branch, w_gate, b_gate, w_o, g_ffn2, w_ffn2_in, w_ffn2_out):
    bf = lambda w: w.astype(BF16)
    depth = g_ffn1.shape[0]
    rows = lambda v: [v[l].reshape(1, -1) for l in range(depth)]
    return dict(
        g_ffn1=rows(g_ffn1), w_ffn1_in=bf(w_ffn1_in), w_ffn1_out=bf(w_ffn1_out),
        g_mix=rows(g_mix), w_in=bf(w_in), a_sink=[a_sink[l].reshape(-1) for l in range(depth)],
        c_q_norm=rows(c_q_norm), c_k_norm=rows(c_k_norm),
        wb_a=bf(w_branch[:, :A_Q]), wb_b=bf(w_branch[:, A_Q:A_Q + B_W]), wb_c=bf(w_branch[:, A_Q + B_W:]),
        w_gate=bf(w_gate), b_gate=rows(b_gate), w_o=bf(w_o),
        g_ffn2=rows(g_ffn2), w_ffn2_in=bf(w_ffn2_in), w_ffn2_out=bf(w_ffn2_out),
    )


def _trunk(x, w, g_final, tables):
    bsz, seq, d = x.shape
    t = bsz * seq
    depth = len(w["g_ffn1"])
    gw = B_HEADS_PER_GROUP * HEAD_DIM
    x = x.reshape(t, d)
    for l in range(depth):
        h = _ffn(x, w["g_ffn1"][l], w["w_ffn1_in"], w["w_ffn1_out"], g_final, layer=l, final_norm=False)
        a, b0, b1, b2, cq, ckv, u = _proj(h, w["g_mix"][l], w["w_in"], w["c_q_norm"][l], w["c_k_norm"][l],
                                          tables, bsz, seq, l)
        (ya,) = _band(a.reshape(bsz, 1, seq, A_WIDTH), w["a_sink"][l], W=A_HALF_WINDOW,
                      G=A_Q_HEADS // A_KV_HEADS, HKV=A_KV_HEADS, rows=BAND_ROWS, want_lse=False)
        obs, lss = [], []
        for bg, (window, dil) in zip((b0, b1, b2), B_GROUPS):
            o, ls = _band(bg.reshape(bsz, dil, seq // dil, B_GROUP_WIDTH), None, W=window // (2 * dil),
                          G=1, HKV=B_HEADS_PER_GROUP, rows=BAND_ROWS_DILATED, want_lse=True)
            obs.append(o.reshape(t, gw) if dil == 1 else o)
            lss.append(ls.reshape(t, gw) if dil == 1 else ls)
        yc = _flash(cq.reshape(bsz, seq, C_Q), ckv.reshape(bsz, seq, 2 * C_KV))
        merged = _mix(u, ya.reshape(t, A_Q), obs, lss, yc.reshape(t, C_Q),
                      w["w_gate"], w["b_gate"][l], w["wb_a"], w["wb_b"], w["wb_c"], seq, l)
        h = _out_proj(h, merged, w["w_o"], l)
        x = _ffn(h, w["g_ffn2"][l], w["w_ffn2_in"], w["w_ffn2_out"], g_final, layer=l,
                 final_norm=(l == depth - 1))
    return x.reshape(bsz, seq, d)


def kernel(x_prompt, x_sample, g_ffn1, w_ffn1_in, w_ffn1_out, g_mix, w_in, a_sink, c_q_norm, c_k_norm,
           w_branch, w_gate, b_gate, w_o, g_ffn2, w_ffn2_in, w_ffn2_out, g_final):
    w = _prepare_weights(g_ffn1, w_ffn1_in, w_ffn1_out, g_mix, w_in, a_sink, c_q_norm, c_k_norm,
                         w_branch, w_gate, b_gate, w_o, g_ffn2, w_ffn2_in, w_ffn2_out)
    gf = g_final.reshape(1, -1)
    tables = _rope_tables(max(x_prompt.shape[1], x_sample.shape[1]))
    return _trunk(x_prompt, w, gf, tables), _trunk(x_sample, w, gf, tables)
```

```python
import functools

import jax
import jax.numpy as jnp
from jax import lax
from jax.experimental import pallas as pl
from jax.experimental.pallas import tpu as pltpu

F32 = jnp.float32
BF16 = jnp.bfloat16

D_MODEL = 2048
D_FF = 5632
HEAD_DIM = 128
A_Q_HEADS, A_KV_HEADS, A_HALF_WINDOW = 4, 2, 128
B_GROUPS = ((128, 1), (512, 4), (2048, 16))
B_HEADS_PER_GROUP = 2
C_Q_HEADS, C_KV_HEADS = 6, 2
GRID_W = 64
ROPE_THETA = 10000.0
N_BRANCH = 3
NORM_EPS = 1e-6
MASK_VALUE = -1e30
Q_SCALE = HEAD_DIM ** -0.5
LOG2_E = 1.4426950408889634
LN_2 = 0.6931471805599453

A_Q = A_Q_HEADS * HEAD_DIM
A_KV = A_KV_HEADS * HEAD_DIM
B_W = B_HEADS_PER_GROUP * len(B_GROUPS) * HEAD_DIM
C_Q = C_Q_HEADS * HEAD_DIM
C_KV = C_KV_HEADS * HEAD_DIM
A_WIDTH = A_Q + 2 * A_KV
B_GROUP_WIDTH = 3 * B_HEADS_PER_GROUP * HEAD_DIM
IN_WIDTH = A_WIDTH + 3 * B_W + C_Q + 2 * C_KV

VMEM_LIMIT_BYTES = 58 * 1024 * 1024

TOKEN_BLOCK = 512
FFN_TOKEN_BLOCK = 1024
FF_CHUNK = 512
PROJ_CHUNK = 512
PROJ_STAGE_SLOTS = 4
MIX_CHUNK = 512
MIX_TOKEN_BLOCK = 1024
OUT_TOKEN_BLOCK = 1024
BAND_ROWS = 2048
BAND_ROWS_DILATED = 4096
FLASH_Q_ROWS = 256
FLASH_K_ROWS = 512
FLASH_MAX_UNROLL = 16


def _params(*sem):
    return pltpu.CompilerParams(dimension_semantics=sem, vmem_limit_bytes=VMEM_LIMIT_BYTES)


def _rms(x, g):
    return x * lax.rsqrt(jnp.mean(x * x, axis=-1, keepdims=True) + NORM_EPS) * g


def _ffn_kernel(x_ref, g_ref, wg_ref, wu_ref, wo_ref, gf_ref, o_ref, n_ref, *, final_norm):
    j = pl.program_id(1)

    @pl.when(j == 0)
    def _():
        x = x_ref[...]
        n_ref[...] = _rms(x, g_ref[...]).astype(BF16)
        o_ref[...] = x

    n = n_ref[...]
    gate = jnp.dot(n, wg_ref[...], preferred_element_type=F32)
    up = jnp.dot(n, wu_ref[...], preferred_element_type=F32)
    a = (gate * jax.nn.sigmoid(gate) * (0.5 * up)).astype(BF16)
    o_ref[...] += jnp.dot(a, wo_ref[...], preferred_element_type=F32)

    if final_norm:
        @pl.when(j == pl.num_programs(1) - 1)
        def _():
            o_ref[...] = _rms(o_ref[...], gf_ref[...])


def _ffn(x, g, w_in, w_out, g_final, *, layer, final_norm):
    t, d = x.shape
    tm = min(FFN_TOKEN_BLOCK, t)
    nf = D_FF // FF_CHUNK
    return pl.pallas_call(
        functools.partial(_ffn_kernel, final_norm=final_norm),
        grid=(t // tm, nf),
        in_specs=[
            pl.BlockSpec((tm, d), lambda i, j: (i, 0)),
            pl.BlockSpec((1, d), lambda i, j: (0, 0)),
            pl.BlockSpec((None, d, FF_CHUNK), lambda i, j: (layer, 0, j)),
            pl.BlockSpec((None, d, FF_CHUNK), lambda i, j: (layer, 0, j + nf)),
            pl.BlockSpec((None, FF_CHUNK, d), lambda i, j: (layer, j, 0)),
            pl.BlockSpec((1, d), lambda i, j: (0, 0)),
        ],
        out_specs=pl.BlockSpec((tm, d), lambda i, j: (i, 0)),
        out_shape=jax.ShapeDtypeStruct((t, d), F32),
        scratch_shapes=[pltpu.VMEM((tm, d), BF16)],
        compiler_params=_params("parallel", "arbitrary"),
        name="ffn_final" if final_norm else "ffn",
    )(x, g, w_in, w_in, w_out, g_final)


def _rope(y, cos, sin):
    return y * cos + pltpu.roll(y, HEAD_DIM // 2, 1) * sin


def _axial_rope(y, cos, sin_lo, sin_hi):
    q = HEAD_DIM // 4
    return y * cos + pltpu.roll(y, HEAD_DIM - q, 1) * sin_lo + pltpu.roll(y, q, 1) * sin_hi


def _proj_layout():
    heads = []
    hpg = B_HEADS_PER_GROUP
    for h in range(A_Q_HEADS):
        heads.append(("rope_q", 0, h * HEAD_DIM))
    for h in range(A_KV_HEADS):
        heads.append(("rope", 0, A_Q + h * HEAD_DIM))
    for h in range(A_KV_HEADS):
        heads.append(("plain", 0, A_Q + A_KV + h * HEAD_DIM))
    for part, kind in enumerate(("rope_q", "rope", "plain")):
        for h in range(hpg * len(B_GROUPS)):
            heads.append((kind, 1 + h // hpg, (part * hpg + h % hpg) * HEAD_DIM))
    for h in range(C_Q_HEADS):
        heads.append(("axial_q", 4, h * HEAD_DIM))
    for h in range(C_KV_HEADS):
        heads.append(("axial_k", 5, h * HEAD_DIM))
    for h in range(C_KV_HEADS):
        heads.append(("plain", 5, C_KV + h * HEAD_DIM))
    assert len(heads) * HEAD_DIM == IN_WIDTH
    return heads


def _proj_kernel(h_ref, g_ref, w_ref, gq_ref, gk_ref, cos_ref, sin_ref, cosc_ref, sinlo_ref, sinhi_ref,
                 a_ref, b0_ref, b1_ref, b2_ref, cq_ref, ckv_ref, u_ref, stage_ref):
    outs = (a_ref, b0_ref, b1_ref, b2_ref, cq_ref, ckv_ref)
    u = _rms(h_ref[...], g_ref[...]).astype(BF16)
    u_ref[...] = u
    layout = _proj_layout()
    heads_per_chunk = PROJ_CHUNK // HEAD_DIM
    staged = 0
    nchunk = IN_WIDTH // PROJ_CHUNK
    cost = {"plain": 0, "rope": 1, "rope_q": 1, "axial_q": 2, "axial_k": 2}
    order = sorted(range(nchunk), key=lambda c: -sum(
        cost[layout[c * heads_per_chunk + hh][0]] for hh in range(heads_per_chunk)))
    for c in order:
        p = jnp.dot(u, w_ref[:, c * PROJ_CHUNK:(c + 1) * PROJ_CHUNK], preferred_element_type=F32)
        for hh in range(heads_per_chunk):
            kind, oi, col = layout[c * heads_per_chunk + hh]
            y = p[:, hh * HEAD_DIM:(hh + 1) * HEAD_DIM]
            if kind in ("rope", "rope_q"):
                y = _rope(y, cos_ref[...], sin_ref[...])
            elif kind in ("axial_q", "axial_k"):
                y = _rms(y, gq_ref[...] if kind == "axial_q" else gk_ref[...])
                y = _axial_rope(y, cosc_ref[...], sinlo_ref[...], sinhi_ref[...])
            if kind.endswith("_q"):
                y = y * (Q_SCALE * LOG2_E)
            dil = B_GROUPS[oi - 1][1] if 1 <= oi <= len(B_GROUPS) else 1
            if dil == 1:
                outs[oi][:, col:col + HEAD_DIM] = y.astype(BF16)
            else:
                slot = staged % PROJ_STAGE_SLOTS
                staged += 1
                stage_ref[slot] = y
                n = y.shape[0] // dil
                for r in range(dil):
                    outs[oi][r, :, col:col + HEAD_DIM] = stage_ref[slot, pl.ds(r, n, stride=dil), :].astype(BF16)


def _proj(h, g, w, gq, gk, tables, bsz, seq, layer):
    t, d = h.shape
    tm = min(TOKEN_BLOCK, seq)
    nseq = seq // tm
    row = lambda i: (i, 0)
    const = lambda i: (0, 0)
    pos = lambda i: (i % nseq, 0)
    out_specs, out_shape = [], []
    for width, dil in ((A_WIDTH, 1),) + tuple((B_GROUP_WIDTH, dl) for _, dl in B_GROUPS) + ((C_Q, 1), (2 * C_KV, 1)):
        if dil == 1:
            out_specs.append(pl.BlockSpec((tm, width), row))
            out_shape.append(jax.ShapeDtypeStruct((t, width), BF16))
        else:
            out_specs.append(pl.BlockSpec((None, dil, tm // dil, width), lambda i: (i // nseq, 0, i % nseq, 0)))
            out_shape.append(jax.ShapeDtypeStruct((bsz, dil, seq // dil, width), BF16))
    out_specs.append(pl.BlockSpec((tm, d), row))
    out_shape.append(jax.ShapeDtypeStruct((t, d), BF16))
    return pl.pallas_call(
        _proj_kernel,
        grid=(t // tm,),
        in_specs=[
            pl.BlockSpec((tm, d), row),
            pl.BlockSpec((1, d), const),
            pl.BlockSpec((None, d, IN_WIDTH), lambda i: (layer, 0, 0), pipeline_mode=pl.Buffered(1)),
            pl.BlockSpec((1, HEAD_DIM), const),
            pl.BlockSpec((1, HEAD_DIM), const),
        ] + [pl.BlockSpec((tm, HEAD_DIM), pos)] * 5,
        out_specs=out_specs,
        out_shape=out_shape,
        scratch_shapes=[pltpu.VMEM((PROJ_STAGE_SLOTS, tm, HEAD_DIM), F32)],
        compiler_params=_params("parallel"),
        name="proj",
    )(h, g, w, gq, gk, *tables)


def _band_kernel(*refs, W, G, HKV, TQ, L, RB, has_sink, want_lse):
    refs = list(refs)
    main_ref, prev_ref, next_ref = refs[:3]
    refs = refs[3:]
    sink_ref = refs.pop(0) if has_sink else None
    o_ref = refs.pop(0)
    lse_ref = refs.pop(0) if want_lse else None

    i = pl.program_id(2)
    QB = min(HEAD_DIM, TQ)
    KW = QB + 2 * W
    nb = TQ // QB

    def head(ref, rr, idx):
        return ref[rr, :, idx * HEAD_DIM:(idx + 1) * HEAD_DIM]

    qs, ks, vs = [], [], []
    for rr in range(RB):
        for h in range(HKV):
            kh, vh = HKV * G + h, HKV * G + HKV + h
            kcat = jnp.concatenate([head(r_, rr, kh) for r_ in (prev_ref, main_ref, next_ref)], axis=0)
            vcat = jnp.concatenate([head(r_, rr, vh) for r_ in (prev_ref, main_ref, next_ref)], axis=0)
            for b in range(nb):
                qg = [main_ref[rr, b * QB:(b + 1) * QB, (h * G + g) * HEAD_DIM:(h * G + g + 1) * HEAD_DIM]
                      for g in range(G)]
                qs.append(jnp.concatenate(qg, axis=0) if G > 1 else qg[0])
                ks.append(kcat[b * QB:b * QB + KW])
                vs.append(vcat[b * QB:b * QB + KW])
    q3, k3, v3 = jnp.stack(qs), jnp.stack(ks), jnp.stack(vs)
    nbatch = RB * HKV * nb

    qrow = lax.broadcasted_iota(jnp.int32, (1, G * QB, KW), 1) % QB
    kcol = lax.broadcasted_iota(jnp.int32, (1, G * QB, KW), 2)
    in_band = jnp.abs(kcol - W - qrow) <= W
    blk = lax.broadcasted_iota(jnp.int32, (nbatch, 1, KW), 0) % nb
    kpos = lax.broadcasted_iota(jnp.int32, (nbatch, 1, KW), 2) + blk * QB + (i * TQ - W)
    end_bias = jnp.where(jnp.logical_and(kpos >= 0, kpos < L), 0.0, MASK_VALUE)

    s = jnp.einsum("bqd,bkd->bqk", q3, k3, preferred_element_type=F32)
    s = jnp.where(in_band, s, MASK_VALUE) + end_bias
    m = jnp.max(s, axis=-1, keepdims=True)
    p = jnp.exp2(s - m)
    l = jnp.sum(p, axis=-1, keepdims=True)
    if has_sink:
        sink = jnp.concatenate([
            jnp.concatenate([jnp.full((nb, QB, 1), sink_ref[h * G + g] * LOG2_E, F32) for g in range(G)], axis=1)
            for _ in range(RB) for h in range(HKV)], axis=0)
        l = l + jnp.exp2(sink - m)
    o = jnp.einsum("bqk,bkd->bqd", p.astype(BF16), v3, preferred_element_type=F32) / l
    if want_lse:
        lse = jnp.broadcast_to(m * LN_2 + jnp.log(l), o.shape)
    for rr in range(RB):
        for h in range(HKV):
            for b in range(nb):
                src = (rr * HKV + h) * nb + b
                for g in range(G):
                    col = (h * G + g) * HEAD_DIM
                    dst = (rr, slice(b * QB, (b + 1) * QB), slice(col, col + HEAD_DIM))
                    o_ref[dst] = o[src, g * QB:(g + 1) * QB].astype(o_ref.dtype)
                    if want_lse:
                        lse_ref[dst] = lse[src, g * QB:(g + 1) * QB]


def _band(x, sink, *, W, G, HKV, rows, want_lse):
    bsz, R, L, cw = x.shape
    assert cw == (HKV * G + 2 * HKV) * HEAD_DIM
    ow = HKV * G * HEAD_DIM
    tq = min(rows, L)
    nblk, per = L // tq, tq // W
    rb = max(1, min(R, rows // tq))
    assert R % rb == 0
    has_sink = sink is not None
    in_specs = [
        pl.BlockSpec((None, rb, tq, cw), lambda b, r, i: (b, r, i, 0)),
        pl.BlockSpec((None, rb, W, cw), lambda b, r, i: (b, r, jnp.maximum(i * per - 1, 0), 0)),
        pl.BlockSpec((None, rb, W, cw), lambda b, r, i: (b, r, jnp.minimum((i + 1) * per, L // W - 1), 0)),
    ]
    args = [x, x, x]
    if has_sink:
        in_specs.append(pl.BlockSpec(memory_space=pltpu.SMEM))
        args.append(sink)
    out_spec = pl.BlockSpec((None, rb, tq, ow), lambda b, r, i: (b, r, i, 0))
    out_specs, out_shape = [out_spec], [jax.ShapeDtypeStruct((bsz, R, L, ow), BF16)]
    if want_lse:
        out_specs.append(out_spec)
        out_shape.append(jax.ShapeDtypeStruct((bsz, R, L, ow), F32))
    return pl.pallas_call(
        functools.partial(_band_kernel, W=W, G=G, HKV=HKV, TQ=tq, L=L, RB=rb, has_sink=has_sink,
                          want_lse=want_lse),
        grid=(bsz, R // rb, nblk),
        in_specs=in_specs,
        out_specs=out_specs,
        out_shape=out_shape,
        compiler_params=_params("parallel", "parallel", "parallel"),
        name="band_sink" if has_sink else f"band_dil{R}",
    )(*args)


ONES_ROWS = 16


def _flash_kernel(q_ref, qn_ref, k_ref, v_ref, o_ref, qs_ref, vt_ref, s_ref, m_ref, acc_ref,
                  *, TQ, TK, G, S, UNROLL):
    nchunks = S // TK
    ntrips = nchunks // UNROLL
    first_block = pl.program_id(2) == 0

    def rows(chunk):
        return pl.ds(pl.multiple_of(chunk * TK, TK), TK)

    @pl.when(first_block)
    def _():
        vt_ref[HEAD_DIM:, :] = jnp.ones((ONES_ROWS, S), BF16)

        def fill(c, carry):
            vt_ref[:HEAD_DIM, rows(c)] = v_ref[0, rows(c), :].astype(F32).T.astype(BF16)
            return carry

        lax.fori_loop(0, nchunks, fill, 0)

    for g in range(G):
        qs_ref[0, g * TQ:(g + 1) * TQ, :] = q_ref[0, :, g * HEAD_DIM:(g + 1) * HEAD_DIM]
        qs_ref[1, g * TQ:(g + 1) * TQ, :] = qn_ref[0, :, g * HEAD_DIM:(g + 1) * HEAD_DIM]
    m_ref[...] = jnp.full(m_ref.shape, MASK_VALUE, F32)
    acc_ref[...] = jnp.zeros(acc_ref.shape, F32)

    def scores(chunk, slot, which=0):
        s_ref[slot] = lax.dot_general(k_ref[0, rows(chunk), :], qs_ref[which], (((1,), (1,)), ((), ())),
                                      preferred_element_type=F32)

    def accumulate(chunk, slot):
        s = s_ref[slot]
        m_prev = m_ref[...]
        m_next = jnp.maximum(m_prev, jnp.max(s, axis=0, keepdims=True))
        alpha = jnp.exp2(m_prev - m_next)
        p = jnp.exp2(s - m_next).astype(BF16)
        acc_ref[...] = alpha * acc_ref[...] + jnp.dot(vt_ref[:, rows(chunk)], p, preferred_element_type=F32)
        m_ref[...] = m_next

    @pl.when(first_block)
    def _():
        scores(0, 0)

    def step(j, carry):
        for u in range(UNROLL):
            c = UNROLL * j + u
            if u < UNROLL - 1:
                scores(c + 1, (u + 1) % 2)
            else:
                handoff = jnp.asarray(j == ntrips - 1, dtype=jnp.int32)
                scores((c + 1) * (1 - handoff), 0, handoff)
            accumulate(c, u % 2)
        return carry

    lax.fori_loop(0, ntrips, step, 0)
    out = (acc_ref[:HEAD_DIM, :] / acc_ref[HEAD_DIM:HEAD_DIM + 1, :]).T
    for g in range(G):
        o_ref[0, :, g * HEAD_DIM:(g + 1) * HEAD_DIM] = out[g * TQ:(g + 1) * TQ].astype(BF16)


def _flash(q, kv):
    bsz, S, _ = q.shape
    G = C_Q_HEADS // C_KV_HEADS
    tq = min(FLASH_Q_ROWS, S)
    tk = min(FLASH_K_ROWS, S // 2)
    nchunks = S // tk
    unroll = max([u for u in (2, 4, 8, FLASH_MAX_UNROLL) if nchunks % u == 0 and nchunks // u >= 2] or [2])
    assert S % tk == 0 and nchunks % unroll == 0
    gw = G * HEAD_DIM
    return pl.pallas_call(
        functools.partial(_flash_kernel, TQ=tq, TK=tk, G=G, S=S, UNROLL=unroll),
        grid=(bsz, C_KV_HEADS, S // tq),
        in_specs=[
            pl.BlockSpec((1, tq, gw), lambda b, h, i: (b, i, h)),
            pl.BlockSpec((1, tq, gw), lambda b, h, i: (b, jnp.minimum(i + 1, S // tq - 1), h)),
            pl.BlockSpec((1, S, HEAD_DIM), lambda b, h, i: (b, 0, h)),
            pl.BlockSpec((1, S, HEAD_DIM), lambda b, h, i: (b, 0, C_KV_HEADS + h)),
        ],
        out_specs=pl.BlockSpec((1, tq, gw), lambda b, h, i: (b, i, h)),
        out_shape=jax.ShapeDtypeStruct((bsz, S, C_Q), BF16),
        scratch_shapes=[
            pltpu.VMEM((2, G * tq, HEAD_DIM), BF16),
            pltpu.VMEM((HEAD_DIM + ONES_ROWS, S), BF16),
            pltpu.VMEM((2, tk, G * tq), F32),
            pltpu.VMEM((1, G * tq), F32),
            pltpu.VMEM((HEAD_DIM + ONES_ROWS, G * tq), F32),
        ],
        compiler_params=_params("parallel", "parallel", "arbitrary"),
        name="flash",
    )(q, q, kv, kv)


def _mix_kernel(u_ref, ya_ref, ob0_ref, ob1_ref, ob2_ref, ls0_ref, ls1_ref, ls2_ref, yc_ref,
                wg0_ref, wg1_ref, wg2_ref, bg0_ref, bg1_ref, bg2_ref, wba_ref, wbb_ref, wbc_ref,
                o_ref, yb_ref, tok_ref):
    c = pl.program_id(1)

    @pl.when(c == 0)
    def _():
        obs = (ob0_ref, ob1_ref, ob2_ref)
        lss = (ls0_ref, ls1_ref, ls2_ref)
        gw = B_HEADS_PER_GROUP * HEAD_DIM
        slot = 0

        def token_major(ref, sl, dil, slot):
            if dil == 1:
                return ref[:, sl].astype(F32)
            n = ref.shape[1]
            for r in range(dil):
                tok_ref[slot, pl.ds(r, n, stride=dil), :] = ref[r, :, sl].astype(F32)
            return tok_ref[slot]

        for j in range(B_HEADS_PER_GROUP):
            sl = slice(j * HEAD_DIM, (j + 1) * HEAD_DIM)
            ls, ob = [], []
            for gi, (_, dil) in enumerate(B_GROUPS):
                ls.append(token_major(lss[gi], sl, dil, slot))
                ob.append(token_major(obs[gi], sl, dil, slot + 1))
                slot += 2 if dil > 1 else 0
            m = jnp.maximum(jnp.maximum(ls[0], ls[1]), ls[2])
            es = [jnp.exp(v - m) for v in ls]
            den = es[0] + es[1] + es[2]
            for gi in range(len(B_GROUPS)):
                wt = es[gi] / den
                yb_ref[:, gi * gw + j * HEAD_DIM:gi * gw + (j + 1) * HEAD_DIM] = (ob[gi] * wt).astype(BF16)

    u = u_ref[...]

    def gate(w_ref, b_ref):
        return jax.nn.sigmoid(jnp.dot(u, w_ref[...], preferred_element_type=F32) + b_ref[...])

    merged = gate(wg0_ref, bg0_ref) * jnp.dot(ya_ref[...], wba_ref[...], preferred_element_type=F32)
    merged += gate(wg1_ref, bg1_ref) * jnp.dot(yb_ref[...], wbb_ref[...], preferred_element_type=F32)
    merged += gate(wg2_ref, bg2_ref) * jnp.dot(yc_ref[...], wbc_ref[...], preferred_element_type=F32)
    o_ref[...] = merged.astype(BF16)


def _mix(u, ya, obs, lss, yc, w_gate, b_gate, wb_a, wb_b, wb_c, seq, layer):
    t, d = u.shape
    tm = min(MIX_TOKEN_BLOCK, seq)
    nseq = seq // tm
    nc = d // MIX_CHUNK
    row = lambda i, c: (i, 0)
    gw = B_HEADS_PER_GROUP * HEAD_DIM
    group_specs = [
        pl.BlockSpec((tm, gw), row) if dil == 1 else
        pl.BlockSpec((None, dil, tm // dil, gw), lambda i, c: (i // nseq, 0, i % nseq, 0))
        for _, dil in B_GROUPS]
    n_stage = 2 * B_HEADS_PER_GROUP * sum(dil > 1 for _, dil in B_GROUPS)
    gate_spec = lambda k: pl.BlockSpec((None, d, MIX_CHUNK), lambda i, c: (layer, 0, k * nc + c))
    bias_spec = lambda k: pl.BlockSpec((1, MIX_CHUNK), lambda i, c: (0, k * nc + c))
    col_spec = lambda rows: pl.BlockSpec((None, rows, MIX_CHUNK), lambda i, c: (layer, 0, c))
    return pl.pallas_call(
        _mix_kernel,
        grid=(t // tm, nc),
        in_specs=[
            pl.BlockSpec((tm, d), row),
            pl.BlockSpec((tm, A_Q), row),
            *group_specs, *group_specs,
            pl.BlockSpec((tm, C_Q), row),
            gate_spec(0), gate_spec(1), gate_spec(2),
            bias_spec(0), bias_spec(1), bias_spec(2),
            col_spec(A_Q), col_spec(B_W), col_spec(C_Q),
        ],
        out_specs=pl.BlockSpec((tm, MIX_CHUNK), lambda i, c: (i, c)),
        out_shape=jax.ShapeDtypeStruct((t, d), BF16),
        scratch_shapes=[pltpu.VMEM((tm, B_W), BF16), pltpu.VMEM((n_stage, tm, HEAD_DIM), F32)],
        compiler_params=_params("parallel", "arbitrary"),
        name="mix",
    )(u, ya, *obs, *lss, yc, w_gate, w_gate, w_gate, b_gate, b_gate, b_gate, wb_a, wb_b, wb_c)


def _out_proj_kernel(h_ref, m_ref, wo_ref, o_ref):
    o_ref[...] = h_ref[...] + jnp.dot(m_ref[...], wo_ref[...], preferred_element_type=F32)


def _out_proj(h, merged, w_o, layer):
    t, d = h.shape
    tm = min(OUT_TOKEN_BLOCK, t)
    row = lambda i: (i, 0)
    return pl.pallas_call(
        _out_proj_kernel,
        grid=(t // tm,),
        in_specs=[
            pl.BlockSpec((tm, d), row),
            pl.BlockSpec((tm, d), row),
            pl.BlockSpec((None, d, d), lambda i: (layer, 0, 0), pipeline_mode=pl.Buffered(1)),
        ],
        out_specs=pl.BlockSpec((tm, d), row),
        out_shape=jax.ShapeDtypeStruct((t, d), F32),
        compiler_params=_params("parallel"),
        name="out_proj",
    )(h, merged, w_o)


def _rope_tables(seq):
    def angles(pos, dim):
        inv = ROPE_THETA ** (-jnp.arange(0, dim, 2, dtype=F32) / dim)
        return pos.astype(F32)[:, None] * inv[None, :]

    pos = jnp.arange(seq)
    ang = angles(pos, HEAD_DIM)
    cos, sin = jnp.cos(ang), jnp.sin(ang)
    ar = angles(pos // GRID_W, HEAD_DIM // 2)
    ac = angles(pos % GRID_W, HEAD_DIM // 2)
    cr, sr, cc, sc = jnp.cos(ar), jnp.sin(ar), jnp.cos(ac), jnp.sin(ac)
    z = jnp.zeros_like(sr)
    return (
        jnp.concatenate([cos, cos], -1),
        jnp.concatenate([-sin, sin], -1),
        jnp.concatenate([cr, cr, cc, cc], -1),
        jnp.concatenate([-sr, z, -sc, z], -1),
        jnp.concatenate([z, sr, z, sc], -1),
    )


def _prepare_weights(g_ffn1, w_ffn1_in, w_ffn1_out, g_mix, w_in, a_sink, c_q_norm, c_k_norm,
                     w_branch, w_gate, b_gate, w_o, g_ffn2, w_ffn2_in, w_ffn2_out):
    bf = lambda w: w.astype(BF16)
    depth = g_ffn1.shape[0]
    rows = lambda v: [v[l].reshape(1, -1) for l in range(depth)]
    return dict(
        g_ffn1=rows(g_ffn1), w_ffn1_in=bf(w_ffn1_in), w_ffn1_out=bf(w_ffn1_out),
        g_mix=rows(g_mix), w_in=bf(w_in), a_sink=[a_sink[l].reshape(-1) for l in range(depth)],
        c_q_norm=rows(c_q_norm), c_k_norm=rows(c_k_norm),
        wb_a=bf(w_branch[:, :A_Q]), wb_b=bf(w_branch[:, A_Q:A_Q + B_W]), wb_c=bf(w_branch[:, A_Q + B_W:]),
        w_gate=bf(w_gate), b_gate=rows(b_gate), w_o=bf(w_o),
        g_ffn2=rows(g_ffn2), w_ffn2_in=bf(w_ffn2_in), w_ffn2_out=bf(w_ffn2_out),
    )


def _trunk(x, w, g_final, tables):
    bsz, seq, d = x.shape
    t = bsz * seq
    depth = len(w["g_ffn1"])
    gw = B_HEADS_PER_GROUP * HEAD_DIM
    x = x.reshape(t, d)
    for l in range(depth):
        h = _ffn(x, w["g_ffn1"][l], w["w_ffn1_in"], w["w_ffn1_out"], g_final, layer=l, final_norm=False)
        a, b0, b1, b2, cq, ckv, u = _proj(h, w["g_mix"][l], w["w_in"], w["c_q_norm"][l], w["c_k_norm"][l],
                                          tables, bsz, seq, l)
        (ya,) = _band(a.reshape(bsz, 1, seq, A_WIDTH), w["a_sink"][l], W=A_HALF_WINDOW,
                      G=A_Q_HEADS // A_KV_HEADS, HKV=A_KV_HEADS, rows=BAND_ROWS, want_lse=False)
        obs, lss = [], []
        for bg, (window, dil) in zip((b0, b1, b2), B_GROUPS):
            o, ls = _band(bg.reshape(bsz, dil, seq // dil, B_GROUP_WIDTH), None, W=window // (2 * dil),
                          G=1, HKV=B_HEADS_PER_GROUP, rows=BAND_ROWS_DILATED, want_lse=True)
            obs.append(o.reshape(t, gw) if dil == 1 else o)
            lss.append(ls.reshape(t, gw) if dil == 1 else ls)
        yc = _flash(cq.reshape(bsz, seq, C_Q), ckv.reshape(bsz, seq, 2 * C_KV))
        merged = _mix(u, ya.reshape(t, A_Q), obs, lss, yc.reshape(t, C_Q),
                      w["w_gate"], w["b_gate"][l], w["wb_a"], w["wb_b"], w["wb_c"], seq, l)
        h = _out_proj(h, merged, w["w_o"], l)
        x = _ffn(h, w["g_ffn2"][l], w["w_ffn2_in"], w["w_ffn2_out"], g_final, layer=l,
                 final_norm=(l == depth - 1))
    return x.reshape(bsz, seq, d)


def kernel(x_prompt, x_sample, g_ffn1, w_ffn1_in, w_ffn1_out, g_mix, w_in, a_sink, c_q_norm, c_k_norm,
           w_branch, w_gate, b_gate, w_o, g_ffn2, w_ffn2_in, w_ffn2_out, g_final):
    w = _prepare_weights(g_ffn1, w_ffn1_in, w_ffn1_out, g_mix, w_in, a_sink, c_q_norm, c_k_norm,
                         w_branch, w_gate, b_gate, w_o, g_ffn2, w_ffn2_in, w_ffn2_out)
    gf = g_final.reshape(1, -1)
    tables = _rope_tables(max(x_prompt.shape[1], x_sample.shape[1]))
    return _trunk(x_prompt, w, gf, tables), _trunk(x_sample, w, gf, tables)
```
